```python
import jax
import jax.numpy as jnp
from jax import lax
import numpy as np


D_MODEL = 1024
BATCH = 2
SEQ = 8192
DEPTH = 2

GRID_W = 64
CTX_LEN = 256

N_HEADS = 8
N_KV_HEADS = 2
HEAD_DIM = 128
Q_PER_KV = N_HEADS // N_KV_HEADS
ROPE_THETA = 10000.0
Q_BLOCK = 128

GLA_HEADS = 4
GLA_DK = D_MODEL // 2 // GLA_HEADS
GLA_DV = D_MODEL // GLA_HEADS
GLA_RANK = 16
GLA_TAU = 16.0
GLA_CHUNK = 64

CONV_CH = D_MODEL
CONV_WIDTH = 31

D_FF = -(-8 * D_MODEL // (3 * 256)) * 256

ALPHA = (2.0 * DEPTH) ** 0.25
BETA = (8.0 * DEPTH) ** -0.25
NORM_EPS = 1e-6

ATT_Q = N_HEADS * HEAD_DIM
ATT_KV = N_KV_HEADS * HEAD_DIM
GLA_K = GLA_HEADS * GLA_DK
GLA_V = GLA_HEADS * GLA_DV
N_BRANCH = 3
MEM_SPLITS = (ATT_KV, ATT_KV, GLA_K, GLA_V, 2 * GLA_RANK)
REST_SPLITS = (ATT_Q, GLA_K, GLA_V, 2 * CONV_CH, N_BRANCH * D_MODEL)
MEM_COLS = sum(MEM_SPLITS)
IN_COLS = MEM_COLS + sum(REST_SPLITS)

kernel_name = 'hybrid_gla_conformer_gqa_diffusion_trunk'


def _split(a, sizes):
    return jnp.split(a, np.cumsum(sizes)[:-1].tolist(), axis=-1)


def _layernorm(x, g=None, b=None):
    xf = x.astype(jnp.float32)
    xc = xf - jnp.mean(xf, -1, keepdims=True)
    y = xc * lax.rsqrt(jnp.mean(xc * xc, -1, keepdims=True) + NORM_EPS)
    if g is not None:
        y = y * g.astype(jnp.float32) + b.astype(jnp.float32)
    return y.astype(x.dtype)


def _rmsnorm(x, g):
    xf = x.astype(jnp.float32)
    y = xf * lax.rsqrt(jnp.mean(xf * xf, -1, keepdims=True) + NORM_EPS)
    return (y * g.astype(jnp.float32)).astype(x.dtype)


def _modulate(x, shift, scale):
    return _layernorm(x) * (1.0 + scale) + shift


def _post(x, f, gate, g, b):
    return _layernorm(ALPHA * x + gate * f, g, b)


def _heads(a, n):
    B, L, _ = a.shape
    return a.reshape(B, L, n, -1).transpose(0, 2, 1, 3)


def _flip(a):
    return a[:, :, ::-1]


def _axial_rope(rows):
    t = jnp.arange(rows * GRID_W)
    row = (t // GRID_W).astype(jnp.float32)
    col = (t % GRID_W).astype(jnp.float32)
    half = HEAD_DIM // 2
    inv = ROPE_THETA ** (-jnp.arange(0, half, 2, dtype=jnp.float32) / half)
    ang = jnp.concatenate([row[:, None] * inv, col[:, None] * inv], axis=-1)
    return jnp.cos(ang), jnp.sin(ang)


def _apply_rope(x, cos, sin):
    xf = x.astype(jnp.float32)
    x1, x2 = xf[..., 0::2], xf[..., 1::2]
    y = jnp.stack([x1 * cos - x2 * sin, x1 * sin + x2 * cos], axis=-1)
    return y.reshape(x.shape).astype(x.dtype)


def _sdpa(q, k, v):
    s = jnp.einsum('bkgqd,bkld->bkgql', q, k, preferred_element_type=jnp.float32) * (HEAD_DIM ** -0.5)
    p = jax.nn.softmax(s, axis=-1)
    return jnp.einsum('bkgql,bkld->bkgqd', p.astype(v.dtype), v)


def _attend_blocks(q, k_all, v_all):
    B, Hk, G, S, Dh = q.shape
    nb = S // Q_BLOCK
    qb = jnp.moveaxis(q.reshape(B, Hk, G, nb, Q_BLOCK, Dh), 3, 0)
    o = lax.map(lambda blk: _sdpa(blk, k_all, v_all), qb)
    return jnp.moveaxis(o, 0, 3).reshape(B, Hk, G, S, Dh)


def _gla_log_gates(glr, w_a2, b_a):
    B, L, _ = glr.shape
    z = jnp.einsum('blnr,nrk->nblk', glr.reshape(B, L, 2, GLA_RANK), w_a2) + b_a[:, None, None, :]
    lg = jax.nn.log_sigmoid(z.astype(jnp.float32)) / GLA_TAU
    lg = lg.reshape(2, B, L, GLA_HEADS, GLA_DK).transpose(0, 1, 3, 2, 4)
    return lg[0], lg[1]


def _gla_scan(q, k, v, logg, s0):
    B, H, L, dk = q.shape
    dv = v.shape[-1]
    n = L // GLA_CHUNK

    def chunks(a):
        a = a.astype(jnp.float32)
        return a.reshape(B, H, n, GLA_CHUNK, a.shape[-1]).transpose(2, 0, 1, 3, 4)

    lower = jnp.tril(jnp.ones((GLA_CHUNK, GLA_CHUNK), dtype=bool))[:, :, None]

    def step(s, inp):
        qc, kc, vc, gc = inp
        b = jnp.cumsum(gc, axis=2)
        rel = jnp.where(lower, b[:, :, :, None, :] - b[:, :, None, :, :], -jnp.inf)
        a = jnp.einsum('bhid,bhjd,bhijd->bhij', qc, kc, jnp.exp(rel))
        o = jnp.einsum('bhid,bhde->bhie', qc * jnp.exp(b), s) + jnp.einsum('bhij,bhje->bhie', a, vc)
        b_end = b[:, :, -1:, :]
        s = jnp.exp(b_end[:, :, 0, :, None]) * s + jnp.einsum('bhjd,bhje->bhde', kc * jnp.exp(b_end - b), vc)
        return s, o

    s_end, o = lax.scan(step, s0, (chunks(q), chunks(k), chunks(v), chunks(logg)))
    return o.transpose(1, 2, 0, 3, 4).reshape(B, H, L, dv).astype(v.dtype), s_end


def _gla_final_state(k, v, logg):
    b = jnp.cumsum(logg, axis=2)
    w = jnp.exp(b[:, :, -1:, :] - b)
    return jnp.einsum('bhld,bhle->bhde', k.astype(jnp.float32) * w, v.astype(jnp.float32))


def _gla_bidir(qg, kg, vg, gf, gb, s_f, s_b):
    o_f, s_f_end = _gla_scan(qg, kg, vg, gf, s_f)
    o_b, s_b_end = _gla_scan(_flip(qg), _flip(kg), _flip(vg), _flip(gb), s_b)
    return o_f + _flip(o_b), s_f_end, s_b_end


def _mem_parts(pm, p):
    k, v, kg, vg, glr = _split(pm, MEM_SPLITS)
    k = _rmsnorm(_heads(k, N_KV_HEADS), p['k_norm'])
    v = _heads(v, N_KV_HEADS)
    gf, gb = _gla_log_gates(glr, p['gla_w_a2'], p['gla_b_a'])
    return k, v, _heads(kg, GLA_HEADS), _heads(vg, GLA_HEADS), gf, gb


def _rest_parts(pr, p):
    q, qg, r, glu, gates = _split(pr, REST_SPLITS)
    B, L, _ = q.shape
    q = _rmsnorm(q.reshape(B, L, N_KV_HEADS, Q_PER_KV, HEAD_DIM), p['q_norm']).transpose(0, 2, 3, 1, 4)
    qg = _heads(qg, GLA_HEADS) * (GLA_DK ** -0.5)
    return q, qg, r, glu, gates


def _conv_module(glu, p):
    a, g = jnp.split(glu, 2, axis=-1)
    y = a * jax.nn.sigmoid(g)
    pad = CONV_WIDTH // 2
    y = lax.conv_general_dilated(y, p['conv_w_dw'].astype(y.dtype), (1,), [(pad, pad)],
                                 dimension_numbers=('NWC', 'WIO', 'NWC'),
                                 feature_group_count=CONV_CH) + p['conv_b_dw']
    y = jax.nn.silu(_layernorm(y, p['conv_ln_g'], p['conv_ln_b']))
    return y @ p['w_conv_o']


def _merge(o_att, o_gla, r, glu, gates, p):
    B, L, _ = r.shape
    y_att = o_att.transpose(0, 3, 1, 2, 4).reshape(B, L, ATT_Q) @ p['w_att_o']
    o_gla = _rmsnorm(o_gla, p['gla_norm']).transpose(0, 2, 1, 3).reshape(B, L, GLA_V)
    y_gla = (o_gla * jax.nn.silu(r)) @ p['w_gla_o']
    y_conv = _conv_module(glu, p)
    g_att, g_gla, g_conv = jnp.split(jax.nn.sigmoid(gates), N_BRANCH, axis=-1)
    return (g_att * y_att + g_gla * y_gla + g_conv * y_conv) @ p['w_out']


def _context_mixer(h, p):
    proj = h @ p['w_in']
    k, v, kg, vg, gf, gb = _mem_parts(proj[..., :MEM_COLS], p)
    q, qg, r, glu, gates = _rest_parts(proj[..., MEM_COLS:], p)
    o_att = _sdpa(q, k, v)
    zero = jnp.zeros(kg.shape[:2] + (GLA_DK, GLA_DV), jnp.float32)
    o_gla, s_f, s_b = _gla_bidir(qg, kg, vg, gf, gb, zero, zero)
    return _merge(o_att, o_gla, r, glu, gates, p), (k, v, s_f, s_b)


def _context_memory(h, p):
    k, v, kg, vg, gf, gb = _mem_parts(h @ p['w_in'][:, :MEM_COLS], p)
    s_f = _gla_final_state(kg, vg, gf)
    s_b = _gla_final_state(_flip(kg), _flip(vg), _flip(gb))
    return k, v, s_f, s_b


def _latent_mixer(h, mem, cos, sin, p):
    proj = h @ p['w_in']
    k, v, kg, vg, gf, gb = _mem_parts(proj[..., :MEM_COLS], p)
    q, qg, r, glu, gates = _rest_parts(proj[..., MEM_COLS:], p)
    k_ctx, v_ctx, s_f, s_b = mem
    k_all = jnp.concatenate([_apply_rope(k, cos, sin), k_ctx], axis=2)
    v_all = jnp.concatenate([v, v_ctx], axis=2)
    o_att = _attend_blocks(_apply_rope(q, cos, sin), k_all, v_all)
    o_gla, _, _ = _gla_bidir(qg, kg, vg, gf, gb, s_f, s_b)
    return _merge(o_att, o_gla, r, glu, gates, p)


def _ffn(h, p):
    return (jax.nn.silu(h @ p['w_ff_gate']) * (h @ p['w_ff_up'])) @ p['w_ff_down']


def setup_inputs(seed: int = 0) -> dict:
    key = jax.random.key(seed)
    ks = jax.random.split(key, 27)

    def nrm(i, shape, scale):
        return jax.random.normal(ks[i], shape, jnp.float32) * scale

    D = D_MODEL
    L = DEPTH
    return {
        'x': nrm(0, (BATCH, SEQ, D), 1.0),
        'c': nrm(1, (BATCH, D), 1.0),
        'ctx': nrm(2, (BATCH, CTX_LEN, D), 1.0),
        'c_ctx': nrm(3, (D,), 1.0),
        'w_ada': nrm(4, (L, D, 6 * D), 0.3 * D ** -0.5),
        'b_ada': nrm(5, (L, 6 * D), 0.02),
        'w_in': nrm(6, (L, D, IN_COLS), D ** -0.5),
        'q_norm': 1.0 + nrm(7, (L, HEAD_DIM), 0.02),
        'k_norm': 1.0 + nrm(8, (L, HEAD_DIM), 0.02),
        'w_att_o': nrm(9, (L, ATT_Q, D), ATT_Q ** -0.5),
        'gla_w_a2': nrm(10, (L, 2, GLA_RANK, GLA_K), GLA_RANK ** -0.5),
        'gla_b_a': nrm(11, (L, 2, GLA_K), 0.5),
        'gla_norm': 1.0 + nrm(12, (L, GLA_DV), 0.02),
        'w_gla_o': nrm(13, (L, GLA_V, D), GLA_V ** -0.5),
        'conv_w_dw': nrm(14, (L, CONV_WIDTH, 1, CONV_CH), CONV_WIDTH ** -0.5),
        'conv_b_dw': nrm(15, (L, CONV_CH), 0.02),
        'conv_ln_g': 1.0 + nrm(16, (L, CONV_CH), 0.02),
        'conv_ln_b': nrm(17, (L, CONV_CH), 0.02),
        'w_conv_o': nrm(18, (L, CONV_CH, D), CONV_CH ** -0.5),
        'w_out': nrm(19, (L, D, D), BETA * D ** -0.5),
        'ln1_g': 1.0 + nrm(20, (L, D), 0.02),
        'ln1_b': nrm(21, (L, D), 0.02),
        'w_ff_gate': nrm(22, (L, D, D_FF), D ** -0.5),
        'w_ff_up': nrm(23, (L, D, D_FF), D ** -0.5),
        'w_ff_down': nrm(24, (L, D_FF, D), BETA * D_FF ** -0.5),
        'ln2_g': 1.0 + nrm(25, (L, D), 0.02),
        'ln2_b': nrm(26, (L, D), 0.02),
    }


def reference(x, c, ctx, c_ctx, w_ada, b_ada, w_in, q_norm, k_norm, w_att_o, gla_w_a2, gla_b_a,
              gla_norm, w_gla_o, conv_w_dw, conv_b_dw, conv_ln_g, conv_ln_b, w_conv_o, w_out,
              ln1_g, ln1_b, w_ff_gate, w_ff_up, w_ff_down, ln2_g, ln2_b):
    rows = x.shape[1] // GRID_W
    cos, sin = _axial_rope(rows)
    silu_c = jax.nn.silu(c)
    silu_cc = jax.nn.silu(c_ctx)
    xc = ctx
    for l in range(DEPTH):
        p = {'w_in': w_in[l], 'q_norm': q_norm[l], 'k_norm': k_norm[l], 'w_att_o': w_att_o[l],
             'gla_w_a2': gla_w_a2[l], 'gla_b_a': gla_b_a[l], 'gla_norm': gla_norm[l],
             'w_gla_o': w_gla_o[l], 'conv_w_dw': conv_w_dw[l], 'conv_b_dw': conv_b_dw[l],
             'conv_ln_g': conv_ln_g[l], 'conv_ln_b': conv_ln_b[l], 'w_conv_o': w_conv_o[l],
             'w_out': w_out[l], 'w_ff_gate': w_ff_gate[l], 'w_ff_up': w_ff_up[l],
             'w_ff_down': w_ff_down[l]}
        last = l == DEPTH - 1
        mod = (silu_c @ w_ada[l] + b_ada[l])[:, None, :]
        sh1, sc1, g1, sh2, sc2, g2 = jnp.split(mod, 6, axis=-1)
        n_cm = 2 if last else 6
        mod_c = jnp.split(silu_cc @ w_ada[l][:, :n_cm * D_MODEL] + b_ada[l][:n_cm * D_MODEL], n_cm)
        hc = _modulate(xc, mod_c[0], mod_c[1])
        if last:
            mem = _context_memory(hc, p)
        else:
            yc, mem = _context_mixer(hc, p)
            xc = _post(xc, yc, mod_c[2], ln1_g[l], ln1_b[l])
            xc = _post(xc, _ffn(_modulate(xc, mod_c[3], mod_c[4]), p), mod_c[5], ln2_g[l], ln2_b[l])
        x = _post(x, _latent_mixer(_modulate(x, sh1, sc1), mem, cos, sin, p), g1, ln1_g[l], ln1_b[l])
        x = _post(x, _ffn(_modulate(x, sh2, sc2), p), g2, ln2_g[l], ln2_b[l])
    return x
```

```python
import functools

import numpy as np
import jax
import jax.numpy as jnp
from jax import lax
from jax.experimental import pallas as pl
from jax.experimental.pallas import tpu as pltpu

F32 = jnp.float32
BF16 = jnp.bfloat16

D_MODEL = 1024
DEPTH = 2
GRID_W = 64
N_HEADS = 8
N_KV_HEADS = 2
HEAD_DIM = 128
Q_PER_KV = N_HEADS // N_KV_HEADS
ROPE_THETA = 10000.0
GLA_HEADS = 4
GLA_DK = 128
GLA_DV = 256
GLA_RANK = 16
GLA_TAU = 16.0
GLA_C = 64
CONV_W = 31
CONV_PAD = CONV_W // 2
HALO = 16
D_FF = 2816
ALPHA = (2.0 * DEPTH) ** 0.25
NORM_EPS = 1e-6
ATT_Q = N_HEADS * HEAD_DIM
ATT_KV = N_KV_HEADS * HEAD_DIM
GLA_K = GLA_HEADS * GLA_DK
GLA_V = GLA_HEADS * GLA_DV
EXP_CLAMP = 80.0

C_Q = 0
C_K = C_Q + ATT_Q
C_V = C_K + ATT_KV
C_GK = C_V + ATT_KV
C_GV = C_GK + GLA_K
C_GQ = C_GV + GLA_V
C_R = C_GQ + GLA_K
C_GLU_A = C_R + GLA_V
C_GLU_G = C_GLU_A + D_MODEL
C_GATES = C_GLU_G + D_MODEL
C_END = C_GATES + 3 * D_MODEL

VMEM_LIMIT = 56 * 1024 * 1024


def _cparams(sem):
    return pltpu.CompilerParams(dimension_semantics=sem, vmem_limit_bytes=VMEM_LIMIT)


def _resident(shape):
    nd = len(shape)
    return pl.BlockSpec(shape, lambda *_: (0,) * nd, pipeline_mode=pl.Buffered(1))


def _ln(x):
    mu = jnp.mean(x, axis=-1, keepdims=True)
    xc = x - mu
    var = jnp.mean(xc * xc, axis=-1, keepdims=True)
    return xc * lax.rsqrt(var + NORM_EPS)


def _silu(x):
    return x * jax.nn.sigmoid(x)


def _dot(a, b):
    return jnp.dot(a, b, preferred_element_type=F32)


def _dot_nt(a, b):
    return lax.dot_general(a, b, (((1,), (1,)), ((), ())), preferred_element_type=F32)


def _dot_tn(a, b):
    return lax.dot_general(a, b, (((0,), (0,)), ((), ())), preferred_element_type=F32)


def _adaln_kernel(c_ref, w_ref, b_ref, o_ref):
    s = _silu(c_ref[...])
    o_ref[0] = jnp.dot(s, w_ref[0], preferred_element_type=F32,
                       precision=lax.Precision.HIGHEST) + b_ref[0]


def _adaln(cvec, w_ada, b_ada):
    L, D, N6 = w_ada.shape
    tn = 1536
    return pl.pallas_call(
        _adaln_kernel,
        out_shape=jax.ShapeDtypeStruct((L, 8, N6), F32),
        grid=(L, N6 // tn),
        in_specs=[pl.BlockSpec((8, D), lambda l, n: (0, 0)),
                  pl.BlockSpec((1, D, tn), lambda l, n: (l, 0, n)),
                  pl.BlockSpec((1, 1, tn), lambda l, n: (l, 0, n))],
        out_specs=pl.BlockSpec((1, 8, tn), lambda l, n: (l, 0, n)),
        compiler_params=_cparams(("arbitrary", "arbitrary")),
        name="adaln",
    )(cvec, w_ada, b_ada.reshape(L, 1, N6))


def _inproj_kernel(x_ref, mod_ref, cos_ref, sin_ref, qn_ref, kn_ref, ba_ref, w_ref, wglr_ref,
                   w2_ref, tril_ref, triu_ref,
                   q_o, k_o, v_o, gk_o, gv_o, gq_o, r_o, y_o, gates_o, bc_o, h_scr, *, rope):
    shift = mod_ref[0, 0:1, :]
    scale = mod_ref[0, 1:2, :]
    h_scr[...] = (_ln(x_ref[0]) * (1.0 + scale) + shift).astype(BF16)

    def mm(c0, width):
        return _dot(h_scr[...], w_ref[:, c0:c0 + width])

    def norm_rope(xh, gain):
        y = xh * lax.rsqrt(jnp.mean(xh * xh, axis=-1, keepdims=True) + NORM_EPS) * gain
        if rope:
            y = y * cos_ref[...] + pltpu.roll(y, HEAD_DIM // 2, 1) * sin_ref[...]
        return y

    for c in range(ATT_Q // 512):
        acc = mm(C_Q + c * 512, 512)
        for g in range(4):
            y = norm_rope(acc[:, g * 128:(g + 1) * 128], qn_ref[...]) * (HEAD_DIM ** -0.5)
            q_o[0, :, c * 512 + g * 128:c * 512 + (g + 1) * 128] = y.astype(BF16)
    acc = mm(C_K, 512)
    for g in range(N_KV_HEADS):
        y = norm_rope(acc[:, g * 128:(g + 1) * 128], kn_ref[...])
        k_o[0, :, g * 128:(g + 1) * 128] = y.astype(BF16)
    v_o[0] = acc[:, ATT_KV:].astype(BF16)

    gk_o[0] = mm(C_GK, GLA_K).astype(BF16)
    for c in range(GLA_V // 512):
        gv_o[0, :, c * 512:(c + 1) * 512] = mm(C_GV + c * 512, 512).astype(BF16)
    gq_o[0] = (mm(C_GQ, GLA_K) * (GLA_DK ** -0.5)).astype(BF16)
    for c in range(GLA_V // 512):
        r_o[0, :, c * 512:(c + 1) * 512] = _silu(mm(C_R + c * 512, 512)).astype(BF16)
    for c in range(D_MODEL // 512):
        a = mm(C_GLU_A + c * 512, 512)
        g = mm(C_GLU_G + c * 512, 512)
        y_o[0, :, c * 512:(c + 1) * 512] = (a * jax.nn.sigmoid(g)).astype(BF16)
    for c in range(3 * D_MODEL // 512):
        gates_o[0, :, c * 512:(c + 1) * 512] = jax.nn.sigmoid(mm(C_GATES + c * 512, 512)).astype(BF16)

    glr = _dot(h_scr[...], wglr_ref[...])
    z = _dot(glr.astype(BF16), w2_ref[...]) + ba_ref[...]
    lg = (jnp.minimum(z, 0.0) - jnp.log(1.0 + jnp.exp(-jnp.abs(z)))) * (1.0 / GLA_TAU)
    for dr, tri_ref in ((0, tril_ref), (1, triu_ref)):
        g = lg[:, dr * GLA_K:(dr + 1) * GLA_K]
        g1 = g.astype(BF16)
        r1 = g - g1.astype(F32)
        g2 = r1.astype(BF16)
        g3 = (r1 - g2.astype(F32)).astype(BF16)
        tri = tri_ref[...]
        bc_o[dr, 0] = _dot(tri, g1) + _dot(tri, g2) + _dot(tri, g3)


def _inproj(x, mod, tabs, p, *, rope, tm):
    B, N, D = x.shape
    nt = N // tm
    row = lambda w: pl.BlockSpec((1, tm, w), lambda b, i: (b, i, 0))
    out_shapes = [jax.ShapeDtypeStruct((B, N, w), BF16)
                  for w in (ATT_Q, ATT_KV, ATT_KV, GLA_K, GLA_V, GLA_K, GLA_V, D_MODEL, 3 * D_MODEL)]
    out_shapes.append(jax.ShapeDtypeStruct((2, B, N, GLA_K), F32))
    out_specs = [row(w) for w in (ATT_Q, ATT_KV, ATT_KV, GLA_K, GLA_V, GLA_K, GLA_V, D_MODEL, 3 * D_MODEL)]
    out_specs.append(pl.BlockSpec((2, 1, tm, GLA_K), lambda b, i: (0, b, i, 0)))
    tril, triu = tabs['tri'][tm]
    return pl.pallas_call(
        functools.partial(_inproj_kernel, rope=rope),
        out_shape=out_shapes,
        grid=(B, nt),
        in_specs=[row(D),
                  pl.BlockSpec((1, 6, D), lambda b, i: (b, 0, 0)),
                  pl.BlockSpec((tm, HEAD_DIM), lambda b, i: (i, 0)),
                  pl.BlockSpec((tm, HEAD_DIM), lambda b, i: (i, 0)),
                  _resident((1, HEAD_DIM)), _resident((1, HEAD_DIM)), _resident((1, 2 * GLA_K)),
                  _resident((D, C_END)), _resident((D, 2 * GLA_RANK)),
                  _resident((2 * GLA_RANK, 2 * GLA_K)),
                  _resident((tm, tm)), _resident((tm, tm))],
        out_specs=out_specs,
        scratch_shapes=[pltpu.VMEM((tm, D), BF16)],
        compiler_params=_cparams(("arbitrary", "arbitrary")),
        name="in_proj_rope" if rope else "in_proj",
    )(x, mod, tabs['cos'], tabs['sin'], p['q_norm'], p['k_norm'], p['b_a'], p['w_in'], p['w_glr'],
      p['w_a2'], tril, triu)


def _attn_kernel(*refs, n_src, src_len, tq, tk):
    q_ref = refs[0]
    kv_refs = refs[1:1 + 2 * n_src]
    o_ref = refs[1 + 2 * n_src]
    m_scr, l_scr, acc_scr = refs[2 + 2 * n_src:]
    q4 = jnp.concatenate([q_ref[0, :, g * 128:(g + 1) * 128] for g in range(Q_PER_KV)], axis=0)
    m_scr[...] = jnp.full(m_scr.shape, -jnp.inf, F32)
    l_scr[...] = jnp.zeros(l_scr.shape, F32)
    acc_scr[...] = jnp.zeros(acc_scr.shape, F32)

    def step(k, v):
        t = k.shape[0]
        s = _dot_nt(q4, k)
        m_prev = m_scr[...]
        m_new = jnp.maximum(m_prev, jnp.max(s, axis=1, keepdims=True))
        alpha = jnp.exp(m_prev - m_new)
        p = jnp.concatenate([jnp.exp(s[:, j * 128:(j + 1) * 128] - m_new) for j in range(t // 128)],
                            axis=1)
        l_scr[...] = alpha * l_scr[...] + jnp.sum(p, axis=1, keepdims=True)
        acc_scr[...] = alpha * acc_scr[...] + _dot(p.astype(BF16), v)
        m_scr[...] = m_new

    for s_i in range(n_src):
        k_ref, v_ref = kv_refs[2 * s_i], kv_refs[2 * s_i + 1]
        n_chunk = src_len[s_i] // tk
        if n_chunk > 1:
            def body(i, carry, k_ref=k_ref, v_ref=v_ref):
                rows = pl.ds(pl.multiple_of(i * tk, tk), tk)
                step(k_ref[0, rows, :], v_ref[0, rows, :])
                return carry
            lax.fori_loop(0, n_chunk, body, 0)
        else:
            step(k_ref[0], v_ref[0])

    o = acc_scr[...] / l_scr[...]
    for g in range(Q_PER_KV):
        o_ref[0, :, g * 128:(g + 1) * 128] = o[g * tq:(g + 1) * tq].astype(BF16)


def _attention(q, srcs, *, tq, tk):
    B, Nq, _ = q.shape
    src_len = tuple(k.shape[1] for k, _ in srcs)
    in_specs = [pl.BlockSpec((1, tq, Q_PER_KV * HEAD_DIM), lambda b, h, i: (b, i, h))]
    args = [q]
    for (k, v), lk in zip(srcs, src_len):
        assert lk % tk == 0 or lk < tk
        in_specs += [pl.BlockSpec((1, lk, HEAD_DIM), lambda b, h, i: (b, 0, h))] * 2
        args += [k, v]
    M = Q_PER_KV * tq
    return pl.pallas_call(
        functools.partial(_attn_kernel, n_src=len(srcs), src_len=src_len, tq=tq, tk=tk),
        out_shape=jax.ShapeDtypeStruct((B, Nq, ATT_Q), BF16),
        grid=(B, N_KV_HEADS, Nq // tq),
        in_specs=in_specs,
        out_specs=pl.BlockSpec((1, tq, Q_PER_KV * HEAD_DIM), lambda b, h, i: (b, i, h)),
        scratch_shapes=[pltpu.VMEM((M, 128), F32)] * 3,
        compiler_params=_cparams(("arbitrary", "arbitrary", "arbitrary")),
        name="attn%d" % len(srcs),
    )(*args)


def _gla_kernel(q_ref, k_ref, v_ref, b_ref, tri_ref, s0_ref, o_ref, send_ref, s_scr, *, n_chunk):
    d = pl.program_id(1)
    j = pl.program_id(2)

    @pl.when(j == 0)
    def _():
        s_scr[...] = s0_ref[0, 0]

    tri = tri_ref[0]

    def body(i, carry):
        c = i + d * (n_chunk - 1 - 2 * i)
        rows = pl.ds(pl.multiple_of(c * GLA_C, GLA_C), GLA_C)
        for h in range(GLA_HEADS):
            kc = slice(h * GLA_DK, (h + 1) * GLA_DK)
            vc = slice(h * GLA_DV, (h + 1) * GLA_DV)
            b = b_ref[0, 0, rows, kc]
            q = q_ref[0, rows, kc].astype(F32)
            k = k_ref[0, rows, kc].astype(F32)
            v = v_ref[0, rows, vc]
            b_end = jnp.min(b, axis=0, keepdims=True)
            half = 0.5 * b_end
            qe = (q * jnp.exp(b)).astype(BF16)
            qt = (q * jnp.exp(jnp.minimum(b - half, EXP_CLAMP))).astype(BF16)
            kt = (k * jnp.exp(jnp.minimum(half - b, EXP_CLAMP))).astype(BF16)
            kh = (k * jnp.exp(b_end - b)).astype(BF16)
            a = _dot_nt(qt, kt) * tri
            s = s_scr[h]
            o = _dot(qe, s.astype(BF16)) + _dot(a.astype(BF16), v)
            o_ref[0, 0, rows, vc] = o.astype(BF16)
            e_col = jnp.transpose(jnp.broadcast_to(jnp.exp(b_end), (GLA_DK, GLA_DK)))
            s_scr[h] = jnp.concatenate([e_col, e_col], axis=1) * s + _dot_tn(kh, v)
        return carry

    lax.fori_loop(0, n_chunk, body, 0)

    @pl.when(j == pl.num_programs(2) - 1)
    def _():
        send_ref[0, 0] = s_scr[...]


def _gla(gq, gk, gv, bc, s0, tri2, *, tb):
    B, N, _ = gq.shape
    nb = N // tb

    def blk(d, jj):
        return jj + d * (nb - 1 - 2 * jj)

    return pl.pallas_call(
        functools.partial(_gla_kernel, n_chunk=tb // GLA_C),
        out_shape=[jax.ShapeDtypeStruct((2, B, N, GLA_V), BF16),
                   jax.ShapeDtypeStruct((2, B, GLA_HEADS, GLA_DK, GLA_DV), F32)],
        grid=(B, 2, nb),
        in_specs=[pl.BlockSpec((1, tb, GLA_K), lambda b, d, j: (b, blk(d, j), 0)),
                  pl.BlockSpec((1, tb, GLA_K), lambda b, d, j: (b, blk(d, j), 0)),
                  pl.BlockSpec((1, tb, GLA_V), lambda b, d, j: (b, blk(d, j), 0)),
                  pl.BlockSpec((1, 1, tb, GLA_K), lambda b, d, j: (d, b, blk(d, j), 0)),
                  pl.BlockSpec((1, GLA_C, GLA_C), lambda b, d, j: (d, 0, 0)),
                  pl.BlockSpec((1, 1, GLA_HEADS, GLA_DK, GLA_DV), lambda b, d, j: (d, b, 0, 0, 0))],
        out_specs=[pl.BlockSpec((1, 1, tb, GLA_V), lambda b, d, j: (d, b, blk(d, j), 0)),
                   pl.BlockSpec((1, 1, GLA_HEADS, GLA_DK, GLA_DV), lambda b, d, j: (d, b, 0, 0, 0))],
        scratch_shapes=[pltpu.VMEM((GLA_HEADS, GLA_DK, GLA_DV), F32)],
        compiler_params=_cparams(("arbitrary", "arbitrary", "arbitrary")),
        name="gla",
    )(gq, gk, gv, bc, tri2, s0)


def _merge_kernel(x_ref, mod_ref, oatt_ref, ogf_ref, ogb_ref, r_ref, y_ref, yp_ref, yn_ref, gates_ref,
                  gn_ref, wdw_ref, bdw_ref, clg_ref, clb_ref, watt_ref, wgla_ref, wconv_ref, wout_ref,
                  lng_ref, lnb_ref, xo_ref, ybuf, cbuf, *, tm):
    i = pl.program_id(1)
    nt = pl.num_programs(1)

    ybuf[0:HALO, :] = jnp.where(i > 0, yp_ref[0].astype(F32), 0.0)
    ybuf[HALO:HALO + tm, :] = y_ref[0].astype(F32)
    ybuf[HALO + tm:, :] = jnp.where(i < nt - 1, yn_ref[0].astype(F32), 0.0)
    R = 32
    for cb in range(D_MODEL // 128):
        lanes = slice(cb * 128, (cb + 1) * 128)
        w = wdw_ref[:, lanes]
        for rb in range(tm // R):
            acc = jnp.zeros((R, 128), F32)
            for t in range(CONV_W):
                off = HALO - CONV_PAD + rb * R + t
                acc = acc + ybuf[off:off + R, lanes] * w[t:t + 1, :]
            cbuf[rb * R:(rb + 1) * R, lanes] = acc
    yc = _ln(cbuf[...] + bdw_ref[...]) * clg_ref[...] + clb_ref[...]
    y_conv = _dot(_silu(yc).astype(BF16), wconv_ref[...])

    y_att = _dot(oatt_ref[0], watt_ref[...])

    parts = []
    for h in range(GLA_HEADS):
        vc = slice(h * GLA_DV, (h + 1) * GLA_DV)
        o = ogf_ref[0, 0, :, vc].astype(F32) + ogb_ref[0, 0, :, vc].astype(F32)
        o = o * lax.rsqrt(jnp.mean(o * o, axis=-1, keepdims=True) + NORM_EPS) * gn_ref[...]
        parts.append((o * r_ref[0, :, vc].astype(F32)).astype(BF16))
    y_gla = _dot(jnp.concatenate(parts, axis=1), wgla_ref[...])

    g_att = gates_ref[0, :, 0:D_MODEL].astype(F32)
    g_gla = gates_ref[0, :, D_MODEL:2 * D_MODEL].astype(F32)
    g_conv = gates_ref[0, :, 2 * D_MODEL:].astype(F32)
    mix = g_att * y_att + g_gla * y_gla + g_conv * y_conv
    f = _dot(mix.astype(BF16), wout_ref[...])
    gate = mod_ref[0, 2:3, :]
    xo_ref[0] = _ln(ALPHA * x_ref[0] + gate * f) * lng_ref[...] + lnb_ref[...]


def _merge(x, mod, o_att, o_gla, r, y, gates, p, *, tm):
    B, N, D = x.shape
    nt = N // tm
    hb = tm // HALO
    row = lambda w: pl.BlockSpec((1, tm, w), lambda b, i: (b, i, 0))
    vec = _resident((1, D))
    return pl.pallas_call(
        functools.partial(_merge_kernel, tm=tm),
        out_shape=jax.ShapeDtypeStruct((B, N, D), F32),
        grid=(B, nt),
        in_specs=[row(D),
                  pl.BlockSpec((1, 6, D), lambda b, i: (b, 0, 0)),
                  row(ATT_Q),
                  pl.BlockSpec((1, 1, tm, GLA_V), lambda b, i: (0, b, i, 0)),
                  pl.BlockSpec((1, 1, tm, GLA_V), lambda b, i: (1, b, i, 0)),
                  row(GLA_V), row(D),
                  pl.BlockSpec((1, HALO, D), lambda b, i: (b, jnp.maximum(i * hb - 1, 0), 0)),
                  pl.BlockSpec((1, HALO, D), lambda b, i: (b, jnp.minimum((i + 1) * hb, nt * hb - 1), 0)),
                  row(3 * D),
                  _resident((1, GLA_DV)), _resident((32, D)), vec, vec, vec,
                  _resident((ATT_Q, D)), _resident((GLA_V, D)), _resident((D, D)), _resident((D, D)),
                  vec, vec],
        out_specs=row(D),
        scratch_shapes=[pltpu.VMEM((tm + 2 * HALO, D), F32), pltpu.VMEM((tm, D), F32)],
        compiler_params=_cparams(("arbitrary", "arbitrary")),
        name="merge",
    )(x, mod, o_att, o_gla, o_gla, r, y, y, y, gates,
      p['gla_norm'], p['conv_w_dw'], p['conv_b_dw'], p['conv_ln_g'], p['conv_ln_b'],
      p['w_att_o'], p['w_gla_o'], p['w_conv_o'], p['w_out'], p['ln1_g'], p['ln1_b'])


def _ffn_kernel(x_ref, mod_ref, wg_ref, wu_ref, wd_ref, lng_ref, lnb_ref, xo_ref, h_scr, a_scr, *, tf):
    x = x_ref[0]
    h_scr[...] = (_ln(x) * (1.0 + mod_ref[0, 4:5, :]) + mod_ref[0, 3:4, :]).astype(BF16)
    for c in range(D_FF // tf):
        cols = slice(c * tf, (c + 1) * tf)
        g = _dot(h_scr[...], wg_ref[:, cols])
        u = _dot(h_scr[...], wu_ref[:, cols])
        a_scr[:, cols] = (_silu(g) * u).astype(BF16)
    f = _dot(a_scr[...], wd_ref[...])
    xo_ref[0] = _ln(ALPHA * x + mod_ref[0, 5:6, :] * f) * lng_ref[...] + lnb_ref[...]


def _ffn(x, mod, p, *, tm):
    B, N, D = x.shape
    row = pl.BlockSpec((1, tm, D), lambda b, i: (b, i, 0))
    return pl.pallas_call(
        functools.partial(_ffn_kernel, tf=D_FF // 4),
        out_shape=jax.ShapeDtypeStruct((B, N, D), F32),
        grid=(B, N // tm),
        in_specs=[row, pl.BlockSpec((1, 6, D), lambda b, i: (b, 0, 0)),
                  _resident((D, D_FF)), _resident((D, D_FF)), _resident((D_FF, D)),
                  _resident((1, D)), _resident((1, D))],
        out_specs=row,
        scratch_shapes=[pltpu.VMEM((tm, D), BF16), pltpu.VMEM((tm, D_FF), BF16)],
        compiler_params=_cparams(("arbitrary", "arbitrary")),
        name="ffn",
    )(x, mod, p['w_ff_gate'], p['w_ff_up'], p['w_ff_down'], p['ln2_g'], p['ln2_b'])


def _head_perm():
    return np.concatenate([np.arange(0, HEAD_DIM, 2), np.arange(1, HEAD_DIM, 2)])


def _tri_blocks(tm):
    idx = np.arange(tm)
    same = (idx[:, None] // GLA_C) == (idx[None, :] // GLA_C)
    tril = same & (idx[None, :] <= idx[:, None])
    triu = same & (idx[None, :] >= idx[:, None])
    return jnp.asarray(tril, BF16), jnp.asarray(triu, BF16)


def _rope_tables(S):
    t = np.arange(S)
    row = (t // GRID_W).astype(np.float32)
    col = (t % GRID_W).astype(np.float32)
    half = HEAD_DIM // 2
    inv = (ROPE_THETA ** (-np.arange(0, half, 2, dtype=np.float32) / half)).astype(np.float32)
    ang = np.concatenate([row[:, None] * inv, col[:, None] * inv], axis=-1).astype(np.float32)
    cos, sin = np.cos(ang), np.sin(ang)
    return (jnp.asarray(np.concatenate([cos, cos], axis=-1), F32),
            jnp.asarray(np.concatenate([-sin, sin], axis=-1), F32))


def _layer_params(l, w_in, q_norm, k_norm, w_att_o, gla_w_a2, gla_b_a, gla_norm, w_gla_o, conv_w_dw,
                  conv_b_dw, conv_ln_g, conv_ln_b, w_conv_o, w_out, ln1_g, ln1_b, w_ff_gate, w_ff_up,
                  w_ff_down, ln2_g, ln2_b):
    perm = _head_perm()
    w = w_in[l]
    o = np.cumsum([0, ATT_KV, ATT_KV, GLA_K, GLA_V, 2 * GLA_RANK, ATT_Q, GLA_K, GLA_V, 2 * D_MODEL,
                   3 * D_MODEL])
    k_w, v_w, gk_w, gv_w, glr_w, q_w, gq_w, r_w, glu_w, gates_w = [w[:, o[i]:o[i + 1]] for i in range(10)]
    q_cols = (np.arange(N_HEADS)[:, None] * HEAD_DIM + perm[None, :]).reshape(-1)
    k_cols = (np.arange(N_KV_HEADS)[:, None] * HEAD_DIM + perm[None, :]).reshape(-1)
    w_main = jnp.concatenate([q_w[:, q_cols], k_w[:, k_cols], v_w, gk_w, gv_w, gq_w, r_w, glu_w, gates_w],
                             axis=1).astype(BF16)
    a2 = gla_w_a2[l]
    zero = jnp.zeros((GLA_RANK, GLA_K), F32)
    w_a2 = jnp.concatenate([jnp.concatenate([a2[0], zero], axis=1),
                            jnp.concatenate([zero, a2[1]], axis=1)], axis=0).astype(BF16)
    vec = lambda a: a[l].reshape(1, -1)
    return {
        'w_in': w_main, 'w_glr': glr_w.astype(BF16), 'w_a2': w_a2,
        'b_a': gla_b_a[l].reshape(1, 2 * GLA_K),
        'q_norm': q_norm[l][perm].reshape(1, HEAD_DIM), 'k_norm': k_norm[l][perm].reshape(1, HEAD_DIM),
        'w_att_o': w_att_o[l].astype(BF16), 'gla_norm': vec(gla_norm), 'w_gla_o': w_gla_o[l].astype(BF16),
        'conv_w_dw': jnp.pad(conv_w_dw[l][:, 0, :], ((0, 32 - CONV_W), (0, 0))),
        'conv_b_dw': vec(conv_b_dw), 'conv_ln_g': vec(conv_ln_g), 'conv_ln_b': vec(conv_ln_b),
        'w_conv_o': w_conv_o[l].astype(BF16), 'w_out': w_out[l].astype(BF16),
        'ln1_g': vec(ln1_g), 'ln1_b': vec(ln1_b),
        'w_ff_gate': w_ff_gate[l].astype(BF16), 'w_ff_up': w_ff_up[l].astype(BF16),
        'w_ff_down': w_ff_down[l].astype(BF16), 'ln2_g': vec(ln2_g), 'ln2_b': vec(ln2_b),
    }


def kernel(x, c, ctx, c_ctx, w_ada, b_ada, w_in, q_norm, k_norm, w_att_o, gla_w_a2, gla_b_a, gla_norm,
           w_gla_o, conv_w_dw, conv_b_dw, conv_ln_g, conv_ln_b, w_conv_o, w_out, ln1_g, ln1_b,
           w_ff_gate, w_ff_up, w_ff_down, ln2_g, ln2_b):
    B, S, D = x.shape
    LC = ctx.shape[1]
    tm_lat = min(256, S)
    tm_ctx = min(256, LC)
    cos, sin = _rope_tables(S)
    tri_sizes = sorted({tm_lat, tm_ctx})
    tabs_lat = {'cos': cos, 'sin': sin, 'tri': {t: _tri_blocks(t) for t in tri_sizes}}
    tabs_ctx = {'cos': cos[:LC], 'sin': sin[:LC], 'tri': tabs_lat['tri']}
    idx = np.arange(GLA_C)
    tri2 = jnp.asarray(np.stack([idx[None, :] <= idx[:, None], idx[None, :] >= idx[:, None]]), F32)

    cvec = jnp.concatenate([c, c_ctx[None, :], jnp.zeros((8 - B - 1, D), F32)], axis=0)
    ada = _adaln(cvec, w_ada, b_ada)
    zero_state = jnp.zeros((2, B, GLA_HEADS, GLA_DK, GLA_DV), F32)

    xc = ctx
    for l in range(DEPTH):
        p = _layer_params(l, w_in, q_norm, k_norm, w_att_o, gla_w_a2, gla_b_a, gla_norm, w_gla_o,
                          conv_w_dw, conv_b_dw, conv_ln_g, conv_ln_b, w_conv_o, w_out, ln1_g, ln1_b,
                          w_ff_gate, w_ff_up, w_ff_down, ln2_g, ln2_b)
        mod = ada[l, :B].reshape(B, 6, D)
        mod_c = jnp.broadcast_to(ada[l, B].reshape(1, 6, D), (B, 6, D))
        last = l == DEPTH - 1

        qc, kc, vc, gkc, gvc, gqc, rc, yc, gatesc, bcc = _inproj(xc, mod_c, tabs_ctx, p, rope=False,
                                                                 tm=tm_ctx)
        o_gla_c, s_ctx = _gla(gqc, gkc, gvc, bcc, zero_state, tri2, tb=tm_ctx)
        if not last:
            o_att_c = _attention(qc, [(kc, vc)], tq=min(128, LC), tk=LC)
            xc = _merge(xc, mod_c, o_att_c, o_gla_c, rc, yc, gatesc, p, tm=tm_ctx)
            xc = _ffn(xc, mod_c, p, tm=tm_ctx)

        q, k, v, gk, gv, gq, r, y, gates, bc = _inproj(x, mod, tabs_lat, p, rope=True, tm=tm_lat)
        o_att = _attention(q, [(k, v), (kc, vc)], tq=min(128, S), tk=min(512, S))
        o_gla, _ = _gla(gq, gk, gv, bc, s_ctx, tri2, tb=min(512, S))
        x = _merge(x, mod, o_att, o_gla, r, y, gates, p, tm=tm_lat)
        x = _ffn(x, mod, p, tm=tm_lat)
    return x
```

```python
import functools

import numpy as np
import jax
import jax.numpy as jnp
from jax import lax
from jax.experimental import pallas as pl
from jax.experimental.pallas import tpu as pltpu

F32 = jnp.float32
BF16 = jnp.bfloat16

D_MODEL = 1024
DEPTH = 2
GRID_W = 64
N_HEADS = 8
N_KV_HEADS = 2
HEAD_DIM = 128
Q_PER_KV = N_HEADS // N_KV_HEADS
ROPE_THETA = 10000.0
GLA_HEADS = 4
GLA_DK = 128
GLA_DV = 256
GLA_RANK = 16
GLA_TAU = 16.0
GLA_C = 64
CONV_W = 31
CONV_PAD = CONV_W // 2
HALO = 16
SUBLANES = 8
D_FF = 2816
ALPHA = (2.0 * DEPTH) ** 0.25
NORM_EPS = 1e-6
ATT_Q = N_HEADS * HEAD_DIM
ATT_KV = N_KV_HEADS * HEAD_DIM
V_EXT = 2 * HEAD_DIM
GLA_K = GLA_HEADS * GLA_DK
GLA_V = GLA_HEADS * GLA_DV
EXP_CLAMP = 80.0
LOG2E = 1.4426950408889634
L_MIN = 1e-20

C_Q = 0
C_K = C_Q + ATT_Q
C_V = C_K + ATT_KV
C_GK = C_V + ATT_KV
C_GV = C_GK + GLA_K
C_GQ = C_GV + GLA_V
C_R = C_GQ + GLA_K
C_GLU_A = C_R + GLA_V
C_GLU_G = C_GLU_A + D_MODEL
C_GATES = C_GLU_G + D_MODEL
C_END = C_GATES + 3 * D_MODEL

VMEM_LIMIT = 56 * 1024 * 1024


def _cparams(sem):
    return pltpu.CompilerParams(dimension_semantics=sem, vmem_limit_bytes=VMEM_LIMIT)


def _resident(shape):
    nd = len(shape)
    return pl.BlockSpec(shape, lambda *_: (0,) * nd, pipeline_mode=pl.Buffered(1))


def _ln(x):
    mu = jnp.mean(x, axis=-1, keepdims=True)
    xc = x - mu
    var = jnp.mean(xc * xc, axis=-1, keepdims=True)
    return xc * lax.rsqrt(var + NORM_EPS)


def _silu(x):
    return x * jax.nn.sigmoid(x)


def _dot(a, b):
    return jnp.dot(a, b, preferred_element_type=F32)


def _dot_nt(a, b):
    return lax.dot_general(a, b, (((1,), (1,)), ((), ())), preferred_element_type=F32)


def _dot_tn(a, b):
    return lax.dot_general(a, b, (((0,), (0,)), ((), ())), preferred_element_type=F32)


def _adaln_kernel(c_ref, w_ref, b_ref, o_ref):
    s = _silu(c_ref[...])
    o_ref[0] = jnp.dot(s, w_ref[0], preferred_element_type=F32,
                       precision=lax.Precision.HIGHEST) + b_ref[0]


def _adaln(cvec, w_ada, b_ada):
    L, D, N6 = w_ada.shape
    tn = 1536
    return pl.pallas_call(
        _adaln_kernel,
        out_shape=jax.ShapeDtypeStruct((L, 8, N6), F32),
        grid=(L, N6 // tn),
        in_specs=[pl.BlockSpec((8, D), lambda l, n: (0, 0)),
                  pl.BlockSpec((1, D, tn), lambda l, n: (l, 0, n)),
                  pl.BlockSpec((1, 1, tn), lambda l, n: (l, 0, n))],
        out_specs=pl.BlockSpec((1, 8, tn), lambda l, n: (l, 0, n)),
        compiler_params=_cparams(("arbitrary", "arbitrary")),
        name="adaln",
    )(cvec, w_ada, b_ada.reshape(L, 1, N6))


def _inproj_kernel(x_ref, mod_ref, cos_ref, sin_ref, qn_ref, kn_ref, ba_ref, w_ref, wglr_ref,
                   w2_ref, tril_ref, triu_ref,
                   q_o, k_o, v_o, gk_o, gv_o, gq_o, r_o, y_o, gates_o, bc_o, h_scr, *, rope):
    shift = mod_ref[0, 0:1, :]
    scale = mod_ref[0, 1:2, :]
    h_scr[...] = (_ln(x_ref[0]) * (1.0 + scale) + shift).astype(BF16)
    tm = h_scr.shape[0]

    def mm(c0, width):
        return _dot(h_scr[...], w_ref[:, c0:c0 + width])

    def norm_rope(xh, gain):
        y = xh * lax.rsqrt(jnp.mean(xh * xh, axis=-1, keepdims=True) + NORM_EPS) * gain
        if rope:
            y = y * cos_ref[...] + pltpu.roll(y, HEAD_DIM // 2, 1) * sin_ref[...]
        return y

    for c in range(ATT_Q // 512):
        acc = mm(C_Q + c * 512, 512)
        for g in range(4):
            y = norm_rope(acc[:, g * 128:(g + 1) * 128], qn_ref[...]) * (LOG2E * HEAD_DIM ** -0.5)
            q_o[0, :, c * 512 + g * 128:c * 512 + (g + 1) * 128] = y.astype(BF16)
    acc = mm(C_K, 512)
    for g in range(N_KV_HEADS):
        y = norm_rope(acc[:, g * 128:(g + 1) * 128], kn_ref[...])
        k_o[0, :, g * 128:(g + 1) * 128] = y.astype(BF16)
        v_o[0, :, g * V_EXT:g * V_EXT + HEAD_DIM] = acc[:, ATT_KV + g * 128:ATT_KV + (g + 1) * 128].astype(BF16)
        v_o[0, :, g * V_EXT + HEAD_DIM:(g + 1) * V_EXT] = jnp.ones((tm, HEAD_DIM), BF16)

    gk_o[0] = mm(C_GK, GLA_K).astype(BF16)
    for c in range(GLA_V // 512):
        gv_o[0, :, c * 512:(c + 1) * 512] = mm(C_GV + c * 512, 512).astype(BF16)
    gq_o[0] = (mm(C_GQ, GLA_K) * (GLA_DK ** -0.5)).astype(BF16)
    for c in range(GLA_V // 512):
        r_o[0, :, c * 512:(c + 1) * 512] = _silu(mm(C_R + c * 512, 512)).astype(BF16)
    for c in range(D_MODEL // 512):
        a = mm(C_GLU_A + c * 512, 512)
        g = mm(C_GLU_G + c * 512, 512)
        y_o[0, :, c * 512:(c + 1) * 512] = (a * jax.nn.sigmoid(g)).astype(BF16)
    for c in range(3 * D_MODEL // 512):
        gates_o[0, :, c * 512:(c + 1) * 512] = jax.nn.sigmoid(mm(C_GATES + c * 512, 512)).astype(BF16)

    glr = _dot(h_scr[...], wglr_ref[...])
    z = _dot(glr.astype(BF16), w2_ref[...]) + ba_ref[...]
    lg = (jnp.minimum(z, 0.0) - jnp.log(1.0 + jnp.exp(-jnp.abs(z)))) * (1.0 / GLA_TAU)
    for dr, tri_ref in ((0, tril_ref), (1, triu_ref)):
        g = lg[:, dr * GLA_K:(dr + 1) * GLA_K]
        g1 = g.astype(BF16)
        r1 = g - g1.astype(F32)
        g2 = r1.astype(BF16)
        g3 = (r1 - g2.astype(F32)).astype(BF16)
        tri = tri_ref[...]
        bc_o[dr, 0] = _dot(tri, g1) + _dot(tri, g2) + _dot(tri, g3)


def _inproj(x, mod, tabs, p, *, rope, tm):
    B, N, D = x.shape
    nt = N // tm
    row = lambda w: pl.BlockSpec((1, tm, w), lambda b, i: (b, i, 0))
    widths = (ATT_Q, ATT_KV, N_KV_HEADS * V_EXT, GLA_K, GLA_V, GLA_K, GLA_V, D_MODEL, 3 * D_MODEL)
    out_shapes = [jax.ShapeDtypeStruct((B, N, w), BF16) for w in widths]
    out_shapes.append(jax.ShapeDtypeStruct((2, B, N, GLA_K), F32))
    out_specs = [row(w) for w in widths]
    out_specs.append(pl.BlockSpec((2, 1, tm, GLA_K), lambda b, i: (0, b, i, 0)))
    tril, triu = tabs['tri'][tm]
    return pl.pallas_call(
        functools.partial(_inproj_kernel, rope=rope),
        out_shape=out_shapes,
        grid=(B, nt),
        in_specs=[row(D),
                  pl.BlockSpec((1, 6, D), lambda b, i: (b, 0, 0)),
                  pl.BlockSpec((tm, HEAD_DIM), lambda b, i: (i, 0)),
                  pl.BlockSpec((tm, HEAD_DIM), lambda b, i: (i, 0)),
                  _resident((1, HEAD_DIM)), _resident((1, HEAD_DIM)), _resident((1, 2 * GLA_K)),
                  _resident((D, C_END)), _resident((D, 2 * GLA_RANK)),
                  _resident((2 * GLA_RANK, 2 * GLA_K)),
                  _resident((tm, tm)), _resident((tm, tm))],
        out_specs=out_specs,
        scratch_shapes=[pltpu.VMEM((tm, D), BF16)],
        compiler_params=_cparams(("arbitrary", "arbitrary")),
        name="in_proj_rope" if rope else "in_proj",
    )(x, mod, tabs['cos'], tabs['sin'], p['q_norm'], p['k_norm'], p['b_a'], p['w_in'], p['w_glr'],
      p['w_a2'], tril, triu)


def _attn_kernel(q_ref, k_ref, v_ref, o_ref, kmax_scr, acc_scr, m_scr, *, n_chunk, tq, tk):
    M = Q_PER_KV * tq

    def rolled(fn):
        def body(i, carry):
            rows = pl.ds(pl.multiple_of(i * tk, tk), tk)
            fn(k_ref[0, rows, :], v_ref[0, rows, :])
            return carry
        lax.fori_loop(0, n_chunk, body, 0)

    @pl.when(pl.program_id(2) == 0)
    def _():
        kmax_scr[...] = jnp.zeros(kmax_scr.shape, F32)

        def knorm(k, v):
            kf = k.astype(F32)
            n2 = jnp.max(jnp.sum(kf * kf, axis=1, keepdims=True), axis=0, keepdims=True)
            kmax_scr[...] = jnp.maximum(kmax_scr[...], n2)
        rolled(knorm)

    q4 = jnp.concatenate([q_ref[0, :, g * 128:(g + 1) * 128] for g in range(Q_PER_KV)], axis=0)
    qf = q4.astype(F32)
    qn2 = jnp.sum(qf * qf, axis=1, keepdims=True)
    shift = jnp.broadcast_to(jnp.sqrt(qn2 * kmax_scr[0:1, 0:1]), (M, 128))

    def probs(s, m):
        return jnp.concatenate([jnp.exp2(s[:, j * 128:(j + 1) * 128] - m).astype(BF16)
                                for j in range(s.shape[1] // 128)], axis=1)

    acc = jnp.zeros((M, V_EXT), F32)
    for c in range(n_chunk):
        rows = slice(c * tk, (c + 1) * tk)
        acc = acc + _dot(probs(_dot_nt(q4, k_ref[0, rows, :]), shift), v_ref[0, rows, :])
    acc_scr[...] = acc

    def write_out():
        acc = acc_scr[...]
        o = acc[:, :HEAD_DIM] / acc[:, HEAD_DIM:HEAD_DIM + 1]
        for g in range(Q_PER_KV):
            o_ref[0, :, g * 128:(g + 1) * 128] = o[g * tq:(g + 1) * tq].astype(BF16)
    write_out()

    l_min = jnp.min(acc_scr[:, HEAD_DIM:HEAD_DIM + 1])

    @pl.when(jnp.logical_not(l_min >= L_MIN))
    def _():
        m_scr[...] = jnp.full(m_scr.shape, -jnp.inf, F32)
        acc_scr[...] = jnp.zeros(acc_scr.shape, F32)

        def online(k, v):
            s = _dot_nt(q4, k)
            m_prev = m_scr[...]
            m_new = jnp.maximum(m_prev, jnp.max(s, axis=1, keepdims=True))
            alpha = jnp.exp2(m_prev - m_new)
            pv = _dot(probs(s, m_new), v)
            acc_scr[...] = jnp.concatenate([alpha, alpha], axis=1) * acc_scr[...] + pv
            m_scr[...] = m_new
        rolled(online)
        write_out()


def _attention(q, k, v, *, tq, tk):
    B, Nq, _ = q.shape
    lk = k.shape[1]
    assert lk % tk == 0
    M = Q_PER_KV * tq
    return pl.pallas_call(
        functools.partial(_attn_kernel, n_chunk=lk // tk, tq=tq, tk=tk),
        out_shape=jax.ShapeDtypeStruct((B, Nq, ATT_Q), BF16),
        grid=(B, N_KV_HEADS, Nq // tq),
        in_specs=[pl.BlockSpec((1, tq, Q_PER_KV * HEAD_DIM), lambda b, h, i: (b, i, h)),
                  pl.BlockSpec((1, lk, HEAD_DIM), lambda b, h, i: (b, 0, h)),
                  pl.BlockSpec((1, lk, V_EXT), lambda b, h, i: (b, 0, h))],
        out_specs=pl.BlockSpec((1, tq, Q_PER_KV * HEAD_DIM), lambda b, h, i: (b, i, h)),
        scratch_shapes=[pltpu.VMEM((SUBLANES, 128), F32), pltpu.VMEM((M, V_EXT), F32),
                        pltpu.VMEM((M, 128), F32)],
        compiler_params=_cparams(("arbitrary", "arbitrary", "arbitrary")),
        name="attn",
    )(q, k, v)


def _gla_kernel(qf_ref, kf_ref, vf_ref, bf_ref, qb_ref, kb_ref, vb_ref, bb_ref, tri_ref, s0_ref,
                of_ref, ob_ref, send_ref, s_scr, *, n_chunk):
    j = pl.program_id(1)

    @pl.when(j == 0)
    def _():
        s_scr[...] = s0_ref[:, 0]

    dirs = ((qf_ref, kf_ref, vf_ref, bf_ref, of_ref), (qb_ref, kb_ref, vb_ref, bb_ref, ob_ref))

    def body(i, carry):
        for d, (q_ref, k_ref, v_ref, b_ref, o_ref) in enumerate(dirs):
            c = i if d == 0 else n_chunk - 1 - i
            rows = pl.ds(pl.multiple_of(c * GLA_C, GLA_C), GLA_C)
            tri = tri_ref[d]
            for h in range(GLA_HEADS):
                kc = slice(h * GLA_DK, (h + 1) * GLA_DK)
                vc = slice(h * GLA_DV, (h + 1) * GLA_DV)
                b = b_ref[0, 0, rows, kc]
                q = q_ref[0, rows, kc].astype(F32)
                k = k_ref[0, rows, kc].astype(F32)
                v = v_ref[0, rows, vc]
                b_end = jnp.min(b, axis=0, keepdims=True)
                half = 0.5 * b_end
                qe = (q * jnp.exp(b)).astype(BF16)
                qt = (q * jnp.exp(jnp.minimum(b - half, EXP_CLAMP))).astype(BF16)
                kt = (k * jnp.exp(jnp.minimum(half - b, EXP_CLAMP))).astype(BF16)
                kh = (k * jnp.exp(b_end - b)).astype(BF16)
                a = _dot_nt(qt, kt) * tri
                s = s_scr[d, h]
                o = _dot(qe, s.astype(BF16)) + _dot(a.astype(BF16), v)
                o_ref[0, rows, vc] = o.astype(BF16)
                e_col = jnp.transpose(jnp.broadcast_to(jnp.exp(b_end), (GLA_DK, GLA_DK)))
                s_scr[d, h] = jnp.concatenate([e_col, e_col], axis=1) * s + _dot_tn(kh, v)
        return carry

    lax.fori_loop(0, n_chunk, body, 0, unroll=2)

    @pl.when(j == pl.num_programs(1) - 1)
    def _():
        send_ref[:, 0] = s_scr[...]


def _gla(gq, gk, gv, bc, s0, tri2, *, tb):
    B, N, _ = gq.shape
    nb = N // tb
    fwd = lambda w: pl.BlockSpec((1, tb, w), lambda b, j: (b, j, 0))
    bwd = lambda w: pl.BlockSpec((1, tb, w), lambda b, j: (b, nb - 1 - j, 0))
    state = pl.BlockSpec((2, 1, GLA_HEADS, GLA_DK, GLA_DV), lambda b, j: (0, b, 0, 0, 0))
    return pl.pallas_call(
        functools.partial(_gla_kernel, n_chunk=tb // GLA_C),
        out_shape=[jax.ShapeDtypeStruct((B, N, GLA_V), BF16), jax.ShapeDtypeStruct((B, N, GLA_V), BF16),
                   jax.ShapeDtypeStruct((2, B, GLA_HEADS, GLA_DK, GLA_DV), F32)],
        grid=(B, nb),
        in_specs=[fwd(GLA_K), fwd(GLA_K), fwd(GLA_V),
                  pl.BlockSpec((1, 1, tb, GLA_K), lambda b, j: (0, b, j, 0)),
                  bwd(GLA_K), bwd(GLA_K), bwd(GLA_V),
                  pl.BlockSpec((1, 1, tb, GLA_K), lambda b, j: (1, b, nb - 1 - j, 0)),
                  _resident((2, GLA_C, GLA_C)), state],
        out_specs=[fwd(GLA_V), bwd(GLA_V), state],
        scratch_shapes=[pltpu.VMEM((2, GLA_HEADS, GLA_DK, GLA_DV), F32)],
        compiler_params=_cparams(("arbitrary", "arbitrary")),
        name="gla",
    )(gq, gk, gv, bc, gq, gk, gv, bc, tri2, s0)


def _merge_kernel(x_ref, mod_ref, oatt_ref, ogf_ref, ogb_ref, r_ref, y_ref, yp_ref, yn_ref, gates_ref,
                  gn_ref, wdw_ref, bdw_ref, clg_ref, clb_ref, watt_ref, wgla_ref, wconv_ref, wout_ref,
                  lng_ref, lnb_ref, xo_ref, ybuf, ysh, cbuf, *, tm):
    i = pl.program_id(1)
    nt = pl.num_programs(1)

    ybuf[0:HALO, :] = jnp.where(i > 0, yp_ref[0].astype(F32), 0.0)
    ybuf[HALO:HALO + tm, :] = y_ref[0].astype(F32)
    ybuf[HALO + tm:, :] = jnp.where(i < nt - 1, yn_ref[0].astype(F32), 0.0)
    span = ysh.shape[1]
    for s in range(1, SUBLANES):
        ysh[s - 1] = ybuf[s:s + span, :]
    for cb in range(D_MODEL // 128):
        lanes = slice(cb * 128, (cb + 1) * 128)
        acc = jnp.zeros((tm, 128), F32)
        for t in range(CONV_W):
            a, s = divmod(HALO - CONV_PAD + t, SUBLANES)
            rows = slice(a * SUBLANES, a * SUBLANES + tm)
            src = ybuf[rows, lanes] if s == 0 else ysh[s - 1, rows, lanes]
            acc = acc + src * wdw_ref[t:t + 1, lanes]
        cbuf[:, lanes] = acc
    yc = _ln(cbuf[...] + bdw_ref[...]) * clg_ref[...] + clb_ref[...]
    y_conv = _dot(_silu(yc).astype(BF16), wconv_ref[...])

    y_att = _dot(oatt_ref[0], watt_ref[...])

    parts = []
    for h in range(GLA_HEADS):
        vc = slice(h * GLA_DV, (h + 1) * GLA_DV)
        o = ogf_ref[0, :, vc].astype(F32) + ogb_ref[0, :, vc].astype(F32)
        o = o * lax.rsqrt(jnp.mean(o * o, axis=-1, keepdims=True) + NORM_EPS) * gn_ref[...]
        parts.append((o * r_ref[0, :, vc].astype(F32)).astype(BF16))
    y_gla = _dot(jnp.concatenate(parts, axis=1), wgla_ref[...])

    g_att = gates_ref[0, :, 0:D_MODEL].astype(F32)
    g_gla = gates_ref[0, :, D_MODEL:2 * D_MODEL].astype(F32)
    g_conv = gates_ref[0, :, 2 * D_MODEL:].astype(F32)
    mix = g_att * y_att + g_gla * y_gla + g_conv * y_conv
    f = _dot(mix.astype(BF16), wout_ref[...])
    gate = mod_ref[0, 2:3, :]
    xo_ref[0] = _ln(ALPHA * x_ref[0] + gate * f) * lng_ref[...] + lnb_ref[...]


def _merge(x, mod, o_att, o_gla_f, o_gla_b, r, y, gates, p, *, tm):
    B, N, D = x.shape
    nt = N // tm
    hb = tm // HALO
    row = lambda w: pl.BlockSpec((1, tm, w), lambda b, i: (b, i, 0))
    vec = _resident((1, D))
    span = tm + 2 * HALO - SUBLANES
    return pl.pallas_call(
        functools.partial(_merge_kernel, tm=tm),
        out_shape=jax.ShapeDtypeStruct((B, N, D), F32),
        grid=(B, nt),
        in_specs=[row(D),
                  pl.BlockSpec((1, 6, D), lambda b, i: (b, 0, 0)),
                  row(ATT_Q), row(GLA_V), row(GLA_V), row(GLA_V), row(D),
                  pl.BlockSpec((1, HALO, D), lambda b, i: (b, jnp.maximum(i * hb - 1, 0), 0)),
                  pl.BlockSpec((1, HALO, D), lambda b, i: (b, jnp.minimum((i + 1) * hb, nt * hb - 1), 0)),
                  row(3 * D),
                  _resident((1, GLA_DV)), _resident((32, D)), vec, vec, vec,
                  _resident((ATT_Q, D)), _resident((GLA_V, D)), _resident((D, D)), _resident((D, D)),
                  vec, vec],
        out_specs=row(D),
        scratch_shapes=[pltpu.VMEM((tm + 2 * HALO, D), F32), pltpu.VMEM((SUBLANES - 1, span, D), F32),
                        pltpu.VMEM((tm, D), F32)],
        compiler_params=_cparams(("arbitrary", "arbitrary")),
        name="merge",
    )(x, mod, o_att, o_gla_f, o_gla_b, r, y, y, y, gates,
      p['gla_norm'], p['conv_w_dw'], p['conv_b_dw'], p['conv_ln_g'], p['conv_ln_b'],
      p['w_att_o'], p['w_gla_o'], p['w_conv_o'], p['w_out'], p['ln1_g'], p['ln1_b'])


def _ffn_kernel(x_ref, mod_ref, wg_ref, wu_ref, wd_ref, lng_ref, lnb_ref, xo_ref, h_scr, a_scr, *, tf):
    x = x_ref[0]
    h_scr[...] = (_ln(x) * (1.0 + mod_ref[0, 4:5, :]) + mod_ref[0, 3:4, :]).astype(BF16)
    for c in range(D_FF // tf):
        cols = slice(c * tf, (c + 1) * tf)
        g = _dot(h_scr[...], wg_ref[:, cols])
        u = _dot(h_scr[...], wu_ref[:, cols])
        a_scr[:, cols] = (_silu(g) * u).astype(BF16)
    f = _dot(a_scr[...], wd_ref[...])
    xo_ref[0] = _ln(ALPHA * x + mod_ref[0, 5:6, :] * f) * lng_ref[...] + lnb_ref[...]


def _ffn(x, mod, p, *, tm):
    B, N, D = x.shape
    row = pl.BlockSpec((1, tm, D), lambda b, i: (b, i, 0))
    return pl.pallas_call(
        functools.partial(_ffn_kernel, tf=D_FF // 4),
        out_shape=jax.ShapeDtypeStruct((B, N, D), F32),
        grid=(B, N // tm),
        in_specs=[row, pl.BlockSpec((1, 6, D), lambda b, i: (b, 0, 0)),
                  _resident((D, D_FF)), _resident((D, D_FF)), _resident((D_FF, D)),
                  _resident((1, D)), _resident((1, D))],
        out_specs=row,
        scratch_shapes=[pltpu.VMEM((tm, D), BF16), pltpu.VMEM((tm, D_FF), BF16)],
        compiler_params=_cparams(("arbitrary", "arbitrary")),
        name="ffn",
    )(x, mod, p['w_ff_gate'], p['w_ff_up'], p['w_ff_down'], p['ln2_g'], p['ln2_b'])


def _deinterleave(w, n_heads):
    d = w.shape[0]
    return w.reshape(d, n_heads, HEAD_DIM // 2, 2).transpose(0, 1, 3, 2).reshape(d, n_heads * HEAD_DIM)


def _tri_blocks(tm):
    idx = np.arange(tm)
    same = (idx[:, None] // GLA_C) == (idx[None, :] // GLA_C)
    tril = same & (idx[None, :] <= idx[:, None])
    triu = same & (idx[None, :] >= idx[:, None])
    return jnp.asarray(tril, BF16), jnp.asarray(triu, BF16)


def _rope_tables(S):
    t = np.arange(S)
    row = (t // GRID_W).astype(np.float32)
    col = (t % GRID_W).astype(np.float32)
    half = HEAD_DIM // 2
    inv = (ROPE_THETA ** (-np.arange(0, half, 2, dtype=np.float32) / half)).astype(np.float32)
    ang = np.concatenate([row[:, None] * inv, col[:, None] * inv], axis=-1).astype(np.float32)
    cos, sin = np.cos(ang), np.sin(ang)
    return (jnp.asarray(np.concatenate([cos, cos], axis=-1), F32),
            jnp.asarray(np.concatenate([-sin, sin], axis=-1), F32))


def _layer_params(l, w_in, q_norm, k_norm, w_att_o, gla_w_a2, gla_b_a, gla_norm, w_gla_o, conv_w_dw,
                  conv_b_dw, conv_ln_g, conv_ln_b, w_conv_o, w_out, ln1_g, ln1_b, w_ff_gate, w_ff_up,
                  w_ff_down, ln2_g, ln2_b):
    w = w_in[l]
    o = np.cumsum([0, ATT_KV, ATT_KV, GLA_K, GLA_V, 2 * GLA_RANK, ATT_Q, GLA_K, GLA_V, 2 * D_MODEL,
                   3 * D_MODEL])
    k_w, v_w, gk_w, gv_w, glr_w, q_w, gq_w, r_w, glu_w, gates_w = [w[:, o[i]:o[i + 1]] for i in range(10)]
    w_main = jnp.concatenate([_deinterleave(q_w, N_HEADS), _deinterleave(k_w, N_KV_HEADS), v_w, gk_w, gv_w,
                              gq_w, r_w, glu_w, gates_w], axis=1).astype(BF16)
    a2 = gla_w_a2[l]
    zero = jnp.zeros((GLA_RANK, GLA_K), F32)
    w_a2 = jnp.concatenate([jnp.concatenate([a2[0], zero], axis=1),
                            jnp.concatenate([zero, a2[1]], axis=1)], axis=0).astype(BF16)
    vec = lambda a: a[l].reshape(1, -1)
    return {
        'w_in': w_main, 'w_glr': glr_w.astype(BF16), 'w_a2': w_a2,
        'b_a': gla_b_a[l].reshape(1, 2 * GLA_K),
        'q_norm': _deinterleave(q_norm[l].reshape(1, HEAD_DIM), 1),
        'k_norm': _deinterleave(k_norm[l].reshape(1, HEAD_DIM), 1),
        'w_att_o': w_att_o[l].astype(BF16), 'gla_norm': vec(gla_norm), 'w_gla_o': w_gla_o[l].astype(BF16),
        'conv_w_dw': jnp.pad(conv_w_dw[l][:, 0, :], ((0, 32 - CONV_W), (0, 0))),
        'conv_b_dw': vec(conv_b_dw), 'conv_ln_g': vec(conv_ln_g), 'conv_ln_b': vec(conv_ln_b),
        'w_conv_o': w_conv_o[l].astype(BF16), 'w_out': w_out[l].astype(BF16),
        'ln1_g': vec(ln1_g), 'ln1_b': vec(ln1_b),
        'w_ff_gate': w_ff_gate[l].astype(BF16), 'w_ff_up': w_ff_up[l].astype(BF16),
        'w_ff_down': w_ff_down[l].astype(BF16), 'ln2_g': vec(ln2_g), 'ln2_b': vec(ln2_b),
    }


def _key_chunk(n_keys):
    return max(t for t in (256, 512, 768, 1024) if n_keys % t == 0)


def kernel(x, c, ctx, c_ctx, w_ada, b_ada, w_in, q_norm, k_norm, w_att_o, gla_w_a2, gla_b_a, gla_norm,
           w_gla_o, conv_w_dw, conv_b_dw, conv_ln_g, conv_ln_b, w_conv_o, w_out, ln1_g, ln1_b,
           w_ff_gate, w_ff_up, w_ff_down, ln2_g, ln2_b):
    B, S, D = x.shape
    LC = ctx.shape[1]
    tm_lat = min(256, S)
    tm_ctx = min(256, LC)
    cos, sin = _rope_tables(S)
    tri_sizes = sorted({tm_lat, tm_ctx})
    tabs_lat = {'cos': cos, 'sin': sin, 'tri': {t: _tri_blocks(t) for t in tri_sizes}}
    tabs_ctx = {'cos': cos[:LC], 'sin': sin[:LC], 'tri': tabs_lat['tri']}
    idx = np.arange(GLA_C)
    tri2 = jnp.asarray(np.stack([idx[None, :] <= idx[:, None], idx[None, :] >= idx[:, None]]), F32)

    cvec = jnp.concatenate([c, c_ctx[None, :], jnp.zeros((8 - B - 1, D), F32)], axis=0)
    ada = _adaln(cvec, w_ada, b_ada)
    zero_state = jnp.zeros((2, B, GLA_HEADS, GLA_DK, GLA_DV), F32)

    xc = ctx
    for l in range(DEPTH):
        p = _layer_params(l, w_in, q_norm, k_norm, w_att_o, gla_w_a2, gla_b_a, gla_norm, w_gla_o,
                          conv_w_dw, conv_b_dw, conv_ln_g, conv_ln_b, w_conv_o, w_out, ln1_g, ln1_b,
                          w_ff_gate, w_ff_up, w_ff_down, ln2_g, ln2_b)
        mod = ada[l, :B].reshape(B, 6, D)
        mod_c = jnp.broadcast_to(ada[l, B].reshape(1, 6, D), (B, 6, D))
        last = l == DEPTH - 1

        qc, kc, vc, gkc, gvc, gqc, rc, yc, gatesc, bcc = _inproj(xc, mod_c, tabs_ctx, p, rope=False,
                                                                 tm=tm_ctx)
        ogf_c, ogb_c, s_ctx = _gla(gqc, gkc, gvc, bcc, zero_state, tri2, tb=tm_ctx)
        if not last:
            o_att_c = _attention(qc, kc, vc, tq=min(128, LC), tk=LC)
            xc = _merge(xc, mod_c, o_att_c, ogf_c, ogb_c, rc, yc, gatesc, p, tm=tm_ctx)
            xc = _ffn(xc, mod_c, p, tm=tm_ctx)

        q, k, v, gk, gv, gq, r, y, gates, bc = _inproj(x, mod, tabs_lat, p, rope=True, tm=tm_lat)
        k_all = jnp.concatenate([k, kc], axis=1)
        v_all = jnp.concatenate([v, vc], axis=1)
        o_att = _attention(q, k_all, v_all, tq=min(128, S), tk=_key_chunk(S + LC))
        ogf, ogb, _ = _gla(gq, gk, gv, bc, s_ctx, tri2, tb=min(512, S))
        x = _merge(x, mod, o_att, ogf, ogb, r, y, gates, p, tm=tm_lat)
        x = _ffn(x, mod, p, tm=tm_lat)
    return x
```

```python
import functools

import numpy as np
import jax
import jax.numpy as jnp
from jax import lax
from jax.experimental import pallas as pl
from jax.experimental.pallas import tpu as pltpu

F32 = jnp.float32
BF16 = jnp.bfloat16

D_MODEL = 1024
DEPTH = 2
GRID_W = 64
N_HEADS = 8
N_KV_HEADS = 2
HEAD_DIM = 128
Q_PER_KV = N_HEADS // N_KV_HEADS
ROPE_THETA = 10000.0
GLA_HEADS = 4
GLA_DK = 128
GLA_DV = 256
GLA_RANK = 16
GLA_TAU = 16.0
GLA_C = 128
CONV_W = 31
CONV_PAD = CONV_W // 2
HALO = 16
SUBLANES = 8
D_FF = 2816
ALPHA = (2.0 * DEPTH) ** 0.25
NORM_EPS = 1e-6
ATT_Q = N_HEADS * HEAD_DIM
ATT_KV = N_KV_HEADS * HEAD_DIM
V_ROWS = HEAD_DIM + 16
GLA_K = GLA_HEADS * GLA_DK
GLA_V = GLA_HEADS * GLA_DV
DECAY_SAFE = 160.0
LOG2E = 1.4426950408889634
L_MIN = 1e-20

C_Q = 0
C_K = C_Q + ATT_Q
C_GK = C_K + ATT_KV
C_GV = C_GK + GLA_K
C_GQ = C_GV + GLA_V
C_R = C_GQ + GLA_K
C_GLU_A = C_R + GLA_V
C_GLU_G = C_GLU_A + D_MODEL
C_GATES = C_GLU_G + D_MODEL
C_END = C_GATES + 3 * D_MODEL

VMEM_LIMIT = 56 * 1024 * 1024


def _cparams(sem):
    return pltpu.CompilerParams(dimension_semantics=sem, vmem_limit_bytes=VMEM_LIMIT)


def _resident(shape):
    nd = len(shape)
    return pl.BlockSpec(shape, lambda *_: (0,) * nd, pipeline_mode=pl.Buffered(1))


def _ln(x):
    mu = jnp.mean(x, axis=-1, keepdims=True)
    xc = x - mu
    var = jnp.mean(xc * xc, axis=-1, keepdims=True)
    return xc * lax.rsqrt(var + NORM_EPS)


def _silu(x):
    return x * jax.nn.sigmoid(x)


def _dot(a, b):
    return jnp.dot(a, b, preferred_element_type=F32)


def _dot_nt(a, b):
    return lax.dot_general(a, b, (((1,), (1,)), ((), ())), preferred_element_type=F32)


def _dot_tn(a, b):
    return lax.dot_general(a, b, (((0,), (0,)), ((), ())), preferred_element_type=F32)


def _adaln_kernel(c_ref, w_ref, b_ref, o_ref):
    s = _silu(c_ref[...])
    o_ref[0] = jnp.dot(s, w_ref[0], preferred_element_type=F32,
                       precision=lax.Precision.HIGHEST) + b_ref[0]


def _adaln(cvec, w_ada, b_ada):
    L, D, N6 = w_ada.shape
    tn = 1536
    return pl.pallas_call(
        _adaln_kernel,
        out_shape=jax.ShapeDtypeStruct((L, 8, N6), F32),
        grid=(L, N6 // tn),
        in_specs=[pl.BlockSpec((8, D), lambda l, n: (0, 0)),
                  pl.BlockSpec((1, D, tn), lambda l, n: (l, 0, n)),
                  pl.BlockSpec((1, 1, tn), lambda l, n: (l, 0, n))],
        out_specs=pl.BlockSpec((1, 8, tn), lambda l, n: (l, 0, n)),
        compiler_params=_cparams(("arbitrary", "arbitrary")),
        name="adaln",
    )(cvec, w_ada, b_ada.reshape(L, 1, N6))


def _inproj_kernel(x_ref, mod_ref, cos_ref, sin_ref, qn_ref, kn_ref, ba_ref, w_ref, wvt_ref, wglr_ref,
                   w2_ref, tril_ref, triu_ref,
                   q_o, k_o, vt_o, gk_o, gv_o, gq_o, r_o, y_o, gates_o, bc_o, h_scr, *, rope):
    shift = mod_ref[0, 0:1, :]
    scale = mod_ref[0, 1:2, :]
    h_scr[...] = (_ln(x_ref[0]) * (1.0 + scale) + shift).astype(BF16)
    tm = h_scr.shape[0]

    def mm(c0, width):
        return _dot(h_scr[...], w_ref[:, c0:c0 + width])

    def norm_rope(xh, gain):
        y = xh * lax.rsqrt(jnp.mean(xh * xh, axis=-1, keepdims=True) + NORM_EPS) * gain
        if rope:
            y = y * cos_ref[...] + pltpu.roll(y, HEAD_DIM // 2, 1) * sin_ref[...]
        return y

    for c in range(ATT_Q // 512):
        acc = mm(C_Q + c * 512, 512)
        for g in range(4):
            y = norm_rope(acc[:, g * 128:(g + 1) * 128], qn_ref[...]) * (LOG2E * HEAD_DIM ** -0.5)
            q_o[0, :, c * 512 + g * 128:c * 512 + (g + 1) * 128] = y.astype(BF16)
    acc = mm(C_K, ATT_KV)
    for g in range(N_KV_HEADS):
        y = norm_rope(acc[:, g * 128:(g + 1) * 128], kn_ref[...])
        k_o[0, :, g * 128:(g + 1) * 128] = y.astype(BF16)
    vt = _dot_nt(wvt_ref[...], h_scr[...])
    for g in range(N_KV_HEADS):
        vt_o[0, g * V_ROWS:g * V_ROWS + HEAD_DIM, :] = vt[g * 128:(g + 1) * 128].astype(BF16)
        vt_o[0, g * V_ROWS + HEAD_DIM:(g + 1) * V_ROWS, :] = jnp.ones((V_ROWS - HEAD_DIM, tm), BF16)

    gk_o[0] = mm(C_GK, GLA_K).astype(BF16)
    for c in range(GLA_V // 512):
        gv_o[0, :, c * 512:(c + 1) * 512] = mm(C_GV + c * 512, 512).astype(BF16)
    gq_o[0] = (mm(C_GQ, GLA_K) * (GLA_DK ** -0.5)).astype(BF16)
    for c in range(GLA_V // 512):
        r_o[0, :, c * 512:(c + 1) * 512] = _silu(mm(C_R + c * 512, 512)).astype(BF16)
    for c in range(D_MODEL // 512):
        a = mm(C_GLU_A + c * 512, 512)
        g = mm(C_GLU_G + c * 512, 512)
        y_o[0, :, c * 512:(c + 1) * 512] = (a * jax.nn.sigmoid(g)).astype(BF16)
    for c in range(3 * D_MODEL // 512):
        gates_o[0, :, c * 512:(c + 1) * 512] = jax.nn.sigmoid(mm(C_GATES + c * 512, 512)).astype(BF16)

    glr = _dot(h_scr[...], wglr_ref[...])
    z = _dot(glr.astype(BF16), w2_ref[...]) + ba_ref[...]
    lg = (jnp.minimum(z, 0.0) - jnp.log(1.0 + jnp.exp(-jnp.abs(z)))) * (1.0 / GLA_TAU)
    for dr, tri_ref in ((0, tril_ref), (1, triu_ref)):
        g = lg[:, dr * GLA_K:(dr + 1) * GLA_K]
        g1 = g.astype(BF16)
        r1 = g - g1.astype(F32)
        g2 = r1.astype(BF16)
        g3 = (r1 - g2.astype(F32)).astype(BF16)
        tri = tri_ref[...]
        bc_o[dr, 0] = _dot(tri, g1) + _dot(tri, g2) + _dot(tri, g3)


def _inproj(x, mod, tabs, p, *, rope, tm):
    B, N, D = x.shape
    nt = N // tm
    row = lambda w: pl.BlockSpec((1, tm, w), lambda b, i: (b, i, 0))
    widths = (ATT_Q, ATT_KV, GLA_K, GLA_V, GLA_K, GLA_V, D_MODEL, 3 * D_MODEL)
    out_shapes = [jax.ShapeDtypeStruct((B, N, w), BF16) for w in widths]
    out_specs = [row(w) for w in widths]
    out_shapes.insert(2, jax.ShapeDtypeStruct((B, N_KV_HEADS * V_ROWS, N), BF16))
    out_specs.insert(2, pl.BlockSpec((1, N_KV_HEADS * V_ROWS, tm), lambda b, i: (b, 0, i)))
    out_shapes.append(jax.ShapeDtypeStruct((2, B, N, GLA_K), F32))
    out_specs.append(pl.BlockSpec((2, 1, tm, GLA_K), lambda b, i: (0, b, i, 0)))
    tril, triu = tabs['tri'][tm]
    return pl.pallas_call(
        functools.partial(_inproj_kernel, rope=rope),
        out_shape=out_shapes,
        grid=(B, nt),
        in_specs=[row(D),
                  pl.BlockSpec((1, 6, D), lambda b, i: (b, 0, 0)),
                  pl.BlockSpec((tm, HEAD_DIM), lambda b, i: (i, 0)),
                  pl.BlockSpec((tm, HEAD_DIM), lambda b, i: (i, 0)),
                  _resident((1, HEAD_DIM)), _resident((1, HEAD_DIM)), _resident((1, 2 * GLA_K)),
                  _resident((D, C_END)), _resident((ATT_KV, D)), _resident((D, 2 * GLA_RANK)),
                  _resident((2 * GLA_RANK, 2 * GLA_K)),
                  _resident((tm, tm)), _resident((tm, tm))],
        out_specs=out_specs,
        scratch_shapes=[pltpu.VMEM((tm, D), BF16)],
        compiler_params=_cparams(("arbitrary", "arbitrary")),
        name="in_proj_rope" if rope else "in_proj",
    )(x, mod, tabs['cos'], tabs['sin'], p['q_norm'], p['k_norm'], p['b_a'], p['w_in'], p['w_vt'],
      p['w_glr'], p['w_a2'], tril, triu)


def _attn_kernel(*refs, src_len, tq, tk):
    n_src = len(src_len)
    q_ref = refs[0]
    src_refs = refs[1:1 + 2 * n_src]
    o_ref = refs[1 + 2 * n_src]
    k_scr, vt_scr, kmax_scr, acc_scr, m_scr = refs[2 + 2 * n_src:]
    M = Q_PER_KV * tq
    n_chunk = sum(src_len) // tk

    def rolled(fn):
        def body(i, carry):
            rows = pl.ds(pl.multiple_of(i * tk, tk), tk)
            fn(k_scr[rows, :], vt_scr[:, rows])
            return carry
        lax.fori_loop(0, n_chunk, body, 0)

    @pl.when(pl.program_id(2) == 0)
    def _():
        off = 0
        for s_i, n in enumerate(src_len):
            k_scr[off:off + n, :] = src_refs[2 * s_i][0]
            vt_scr[:, off:off + n] = src_refs[2 * s_i + 1][0]
            off += n
        kmax_scr[...] = jnp.zeros(kmax_scr.shape, F32)

        def knorm(k, vt):
            kf = k.astype(F32)
            n2 = jnp.max(jnp.sum(kf * kf, axis=1, keepdims=True), axis=0, keepdims=True)
            kmax_scr[...] = jnp.maximum(kmax_scr[...], n2)
        rolled(knorm)

    q4 = jnp.concatenate([q_ref[0, :, g * 128:(g + 1) * 128] for g in range(Q_PER_KV)], axis=0)
    qf = q4.astype(F32)
    qn2 = _dot_nt(jnp.ones((SUBLANES, HEAD_DIM), BF16), (qf * qf).astype(BF16))[0:1, :]
    shift = jnp.sqrt(qn2 * kmax_scr[0:1, 0:1])

    acc = jnp.zeros((V_ROWS, M), F32)
    for c in range(n_chunk):
        rows = slice(c * tk, (c + 1) * tk)
        st = _dot_nt(k_scr[rows, :], q4)
        acc = acc + _dot(vt_scr[:, rows], jnp.exp2(st - shift).astype(BF16))
    acc_scr[...] = acc

    def write_out():
        a = acc_scr[...]
        ot = a[:HEAD_DIM, :] / a[HEAD_DIM:HEAD_DIM + 1, :]
        for g in range(Q_PER_KV):
            o_ref[0, :, g * 128:(g + 1) * 128] = jnp.transpose(ot[:, g * tq:(g + 1) * tq]).astype(BF16)
    write_out()

    l_min = jnp.min(acc_scr[HEAD_DIM:HEAD_DIM + 1, :])

    @pl.when(jnp.logical_not(l_min >= L_MIN))
    def _():
        m_scr[...] = jnp.full(m_scr.shape, -jnp.inf, F32)
        acc_scr[...] = jnp.zeros(acc_scr.shape, F32)

        def online(k, vt):
            st = _dot_nt(k, q4)
            m_prev = m_scr[0:1, :]
            m_new = jnp.maximum(m_prev, jnp.max(st, axis=0, keepdims=True))
            alpha = jnp.exp2(m_prev - m_new)
            acc_scr[...] = alpha * acc_scr[...] + _dot(vt, jnp.exp2(st - m_new).astype(BF16))
            m_scr[...] = jnp.broadcast_to(m_new, m_scr.shape)
        rolled(online)
        write_out()


def _attention(q, srcs, *, tq, tk):
    B, Nq, _ = q.shape
    src_len = tuple(k.shape[1] for k, _ in srcs)
    lk = sum(src_len)
    assert lk % tk == 0
    M = Q_PER_KV * tq
    in_specs = [pl.BlockSpec((1, tq, Q_PER_KV * HEAD_DIM), lambda b, h, i: (b, i, h))]
    args = [q]
    for (k, vt), n in zip(srcs, src_len):
        in_specs += [pl.BlockSpec((1, n, HEAD_DIM), lambda b, h, i: (b, 0, h)),
                     pl.BlockSpec((1, V_ROWS, n), lambda b, h, i: (b, h, 0))]
        args += [k, vt]
    return pl.pallas_call(
        functools.partial(_attn_kernel, src_len=src_len, tq=tq, tk=tk),
        out_shape=jax.ShapeDtypeStruct((B, Nq, ATT_Q), BF16),
        grid=(B, N_KV_HEADS, Nq // tq),
        in_specs=in_specs,
        out_specs=pl.BlockSpec((1, tq, Q_PER_KV * HEAD_DIM), lambda b, h, i: (b, i, h)),
        scratch_shapes=[pltpu.VMEM((lk, HEAD_DIM), BF16), pltpu.VMEM((V_ROWS, lk), BF16),
                        pltpu.VMEM((SUBLANES, 128), F32), pltpu.VMEM((V_ROWS, M), F32),
                        pltpu.VMEM((SUBLANES, M), F32)],
        compiler_params=_cparams(("arbitrary", "arbitrary", "arbitrary")),
        name="attn%d" % len(srcs),
    )(*args)


def _gla_kernel(qf_ref, kf_ref, vf_ref, bf_ref, qb_ref, kb_ref, vb_ref, bb_ref, tri_ref, s0_ref,
                of_ref, ob_ref, send_ref, s_scr, *, n_chunk):
    j = pl.program_id(1)

    @pl.when(j == 0)
    def _():
        s_scr[...] = s0_ref[:, 0]

    dirs = ((qf_ref, kf_ref, vf_ref, bf_ref, of_ref), (qb_ref, kb_ref, vb_ref, bb_ref, ob_ref))

    def intra_exact(q, k, b):
        row_id = lax.broadcasted_iota(jnp.int32, (GLA_C, GLA_DK), 0)
        col_id = lax.broadcasted_iota(jnp.int32, (GLA_C, GLA_C), 1)

        def one(jrow, a):
            pick = row_id == jrow
            k_j = jnp.sum(jnp.where(pick, k, 0.0), axis=0, keepdims=True)
            b_j = jnp.sum(jnp.where(pick, b, 0.0), axis=0, keepdims=True)
            col = jnp.sum(q * k_j * jnp.exp(jnp.minimum(b - b_j, 0.0)), axis=1, keepdims=True)
            return jnp.where(col_id == jrow, col, a)
        return lax.fori_loop(0, GLA_C, one, jnp.zeros((GLA_C, GLA_C), F32))

    def make_body(exact):
        def body(i, carry):
            for d, (q_ref, k_ref, v_ref, b_ref, o_ref) in enumerate(dirs):
                c = i if d == 0 else n_chunk - 1 - i
                rows = pl.ds(pl.multiple_of(c * GLA_C, GLA_C), GLA_C)
                tri = tri_ref[d]
                for h in range(GLA_HEADS):
                    kc = slice(h * GLA_DK, (h + 1) * GLA_DK)
                    vc = slice(h * GLA_DV, (h + 1) * GLA_DV)
                    b = b_ref[0, 0, rows, kc]
                    q = q_ref[0, rows, kc].astype(F32)
                    k = k_ref[0, rows, kc].astype(F32)
                    v = v_ref[0, rows, vc]
                    b_end = jnp.min(b, axis=0, keepdims=True)
                    qe = (q * jnp.exp(b)).astype(BF16)
                    kh = k * jnp.exp(b_end - b)
                    if exact:
                        a = intra_exact(q, k, b)
                    else:
                        half = 0.5 * b_end
                        a = _dot_nt((q * jnp.exp(b - half)).astype(BF16),
                                    (k * jnp.exp(half - b)).astype(BF16))
                    a = a * tri
                    av = _dot(jnp.concatenate([a.astype(BF16), jnp.transpose(kh).astype(BF16)], axis=0), v)
                    s = s_scr[d, h]
                    o_ref[0, rows, vc] = (_dot(qe, s.astype(BF16)) + av[:GLA_C]).astype(BF16)
                    e_col = jnp.transpose(jnp.broadcast_to(jnp.exp(b_end), (GLA_DK, GLA_DK)))
                    s_scr[d, h] = jnp.concatenate([e_col, e_col], axis=1) * s + av[GLA_C:]
            return carry
        return body

    safe = jnp.minimum(jnp.min(bf_ref[...]), jnp.min(bb_ref[...])) >= -DECAY_SAFE

    @pl.when(safe)
    def _():
        lax.fori_loop(0, n_chunk, make_body(False), 0, unroll=2)

    @pl.when(jnp.logical_not(safe))
    def _():
        lax.fori_loop(0, n_chunk, make_body(True), 0)

    @pl.when(j == pl.num_programs(1) - 1)
    def _():
        send_ref[:, 0] = s_scr[...]


def _gla(gq, gk, gv, bc, s0, tri2, *, tb):
    B, N, _ = gq.shape
    nb = N // tb
    fwd = lambda w: pl.BlockSpec((1, tb, w), lambda b, j: (b, j, 0))
    bwd = lambda w: pl.BlockSpec((1, tb, w), lambda b, j: (b, nb - 1 - j, 0))
    state = pl.BlockSpec((2, 1, GLA_HEADS, GLA_DK, GLA_DV), lambda b, j: (0, b, 0, 0, 0))
    return pl.pallas_call(
        functools.partial(_gla_kernel, n_chunk=tb // GLA_C),
        out_shape=[jax.ShapeDtypeStruct((B, N, GLA_V), BF16), jax.ShapeDtypeStruct((B, N, GLA_V), BF16),
                   jax.ShapeDtypeStruct((2, B, GLA_HEADS, GLA_DK, GLA_DV), F32)],
        grid=(B, nb),
        in_specs=[fwd(GLA_K), fwd(GLA_K), fwd(GLA_V),
                  pl.BlockSpec((1, 1, tb, GLA_K), lambda b, j: (0, b, j, 0)),
                  bwd(GLA_K), bwd(GLA_K), bwd(GLA_V),
                  pl.BlockSpec((1, 1, tb, GLA_K), lambda b, j: (1, b, nb - 1 - j, 0)),
                  _resident((2, GLA_C, GLA_C)), state],
        out_specs=[fwd(GLA_V), bwd(GLA_V), state],
        scratch_shapes=[pltpu.VMEM((2, GLA_HEADS, GLA_DK, GLA_DV), F32)],
        compiler_params=_cparams(("arbitrary", "arbitrary")),
        name="gla",
    )(gq, gk, gv, bc, gq, gk, gv, bc, tri2, s0)


def _merge_kernel(x_ref, mod_ref, oatt_ref, ogf_ref, ogb_ref, r_ref, y_ref, yp_ref, yn_ref, gates_ref,
                  gn_ref, wdw_ref, bdw_ref, clg_ref, clb_ref, watt_ref, wgla_ref, wconv_ref, wout_ref,
                  lng_ref, lnb_ref, xo_ref, ybuf, ysh, cbuf, *, tm):
    i = pl.program_id(1)
    nt = pl.num_programs(1)

    ybuf[0:HALO, :] = jnp.where(i > 0, yp_ref[0].astype(F32), 0.0)
    ybuf[HALO:HALO + tm, :] = y_ref[0].astype(F32)
    ybuf[HALO + tm:, :] = jnp.where(i < nt - 1, yn_ref[0].astype(F32), 0.0)
    span = ysh.shape[1]
    for s in range(1, SUBLANES):
        ysh[s - 1] = ybuf[s:s + span, :]
    for cb in range(D_MODEL // 128):
        lanes = slice(cb * 128, (cb + 1) * 128)
        acc = jnp.zeros((tm, 128), F32)
        for t in range(CONV_W):
            a, s = divmod(HALO - CONV_PAD + t, SUBLANES)
            rows = slice(a * SUBLANES, a * SUBLANES + tm)
            src = ybuf[rows, lanes] if s == 0 else ysh[s - 1, rows, lanes]
            acc = acc + src * wdw_ref[t:t + 1, lanes]
        cbuf[:, lanes] = acc
    yc = _ln(cbuf[...] + bdw_ref[...]) * clg_ref[...] + clb_ref[...]
    y_conv = _dot(_silu(yc).astype(BF16), wconv_ref[...])

    y_att = _dot(oatt_ref[0], watt_ref[...])

    parts = []
    for h in range(GLA_HEADS):
        vc = slice(h * GLA_DV, (h + 1) * GLA_DV)
        o = ogf_ref[0, :, vc].astype(F32) + ogb_ref[0, :, vc].astype(F32)
        o = o * lax.rsqrt(jnp.mean(o * o, axis=-1, keepdims=True) + NORM_EPS) * gn_ref[...]
        parts.append((o * r_ref[0, :, vc].astype(F32)).astype(BF16))
    y_gla = _dot(jnp.concatenate(parts, axis=1), wgla_ref[...])

    g_att = gates_ref[0, :, 0:D_MODEL].astype(F32)
    g_gla = gates_ref[0, :, D_MODEL:2 * D_MODEL].astype(F32)
    g_conv = gates_ref[0, :, 2 * D_MODEL:].astype(F32)
    mix = g_att * y_att + g_gla * y_gla + g_conv * y_conv
    f = _dot(mix.astype(BF16), wout_ref[...])
    gate = mod_ref[0, 2:3, :]
    xo_ref[0] = _ln(ALPHA * x_ref[0] + gate * f) * lng_ref[...] + lnb_ref[...]


def _merge(x, mod, o_att, o_gla_f, o_gla_b, r, y, gates, p, *, tm):
    B, N, D = x.shape
    nt = N // tm
    hb = tm // HALO
    row = lambda w: pl.BlockSpec((1, tm, w), lambda b, i: (b, i, 0))
    vec = _resident((1, D))
    span = tm + 2 * HALO - SUBLANES
    return pl.pallas_call(
        functools.partial(_merge_kernel, tm=tm),
        out_shape=jax.ShapeDtypeStruct((B, N, D), F32),
        grid=(B, nt),
        in_specs=[row(D),
                  pl.BlockSpec((1, 6, D), lambda b, i: (b, 0, 0)),
                  row(ATT_Q), row(GLA_V), row(GLA_V), row(GLA_V), row(D),
                  pl.BlockSpec((1, HALO, D), lambda b, i: (b, jnp.maximum(i * hb - 1, 0), 0)),
                  pl.BlockSpec((1, HALO, D), lambda b, i: (b, jnp.minimum((i + 1) * hb, nt * hb - 1), 0)),
                  row(3 * D),
                  _resident((1, GLA_DV)), _resident((32, D)), vec, vec, vec,
                  _resident((ATT_Q, D)), _resident((GLA_V, D)), _resident((D, D)), _resident((D, D)),
                  vec, vec],
        out_specs=row(D),
        scratch_shapes=[pltpu.VMEM((tm + 2 * HALO, D), F32), pltpu.VMEM((SUBLANES - 1, span, D), F32),
                        pltpu.VMEM((tm, D), F32)],
        compiler_params=_cparams(("arbitrary", "arbitrary")),
        name="merge",
    )(x, mod, o_att, o_gla_f, o_gla_b, r, y, y, y, gates,
      p['gla_norm'], p['conv_w_dw'], p['conv_b_dw'], p['conv_ln_g'], p['conv_ln_b'],
      p['w_att_o'], p['w_gla_o'], p['w_conv_o'], p['w_out'], p['ln1_g'], p['ln1_b'])


def _ffn_kernel(x_ref, mod_ref, wg_ref, wu_ref, wd_ref, lng_ref, lnb_ref, xo_ref, h_scr, a_scr, *, tf):
    x = x_ref[0]
    h_scr[...] = (_ln(x) * (1.0 + mod_ref[0, 4:5, :]) + mod_ref[0, 3:4, :]).astype(BF16)
    for c in range(D_FF // tf):
        cols = slice(c * tf, (c + 1) * tf)
        g = _dot(h_scr[...], wg_ref[:, cols])
        u = _dot(h_scr[...], wu_ref[:, cols])
        a_scr[:, cols] = (_silu(g) * u).astype(BF16)
    f = _dot(a_scr[...], wd_ref[...])
    xo_ref[0] = _ln(ALPHA * x + mod_ref[0, 5:6, :] * f) * lng_ref[...] + lnb_ref[...]


def _ffn(x, mod, p, *, tm):
    B, N, D = x.shape
    row = pl.BlockSpec((1, tm, D), lambda b, i: (b, i, 0))
    return pl.pallas_call(
        functools.partial(_ffn_kernel, tf=D_FF // 4),
        out_shape=jax.ShapeDtypeStruct((B, N, D), F32),
        grid=(B, N // tm),
        in_specs=[row, pl.BlockSpec((1, 6, D), lambda b, i: (b, 0, 0)),
                  _resident((D, D_FF)), _resident((D, D_FF)), _resident((D_FF, D)),
                  _resident((1, D)), _resident((1, D))],
        out_specs=row,
        scratch_shapes=[pltpu.VMEM((tm, D), BF16), pltpu.VMEM((tm, D_FF), BF16)],
        compiler_params=_cparams(("arbitrary", "arbitrary")),
        name="ffn",
    )(x, mod, p['w_ff_gate'], p['w_ff_up'], p['w_ff_down'], p['ln2_g'], p['ln2_b'])


def _deinterleave(w, n_heads):
    d = w.shape[0]
    return w.reshape(d, n_heads, HEAD_DIM // 2, 2).transpose(0, 1, 3, 2).reshape(d, n_heads * HEAD_DIM)


def _tri_blocks(tm):
    idx = np.arange(tm)
    same = (idx[:, None] // GLA_C) == (idx[None, :] // GLA_C)
    tril = same & (idx[None, :] <= idx[:, None])
    triu = same & (idx[None, :] >= idx[:, None])
    return jnp.asarray(tril, BF16), jnp.asarray(triu, BF16)


def _rope_tables(S):
    t = np.arange(S)
    row = (t // GRID_W).astype(np.float32)
    col = (t % GRID_W).astype(np.float32)
    half = HEAD_DIM // 2
    inv = (ROPE_THETA ** (-np.arange(0, half, 2, dtype=np.float32) / half)).astype(np.float32)
    ang = np.concatenate([row[:, None] * inv, col[:, None] * inv], axis=-1).astype(np.float32)
    cos, sin = np.cos(ang), np.sin(ang)
    return (jnp.asarray(np.concatenate([cos, cos], axis=-1), F32),
            jnp.asarray(np.concatenate([-sin, sin], axis=-1), F32))


def _layer_params(l, w_in, q_norm, k_norm, w_att_o, gla_w_a2, gla_b_a, gla_norm, w_gla_o, conv_w_dw,
                  conv_b_dw, conv_ln_g, conv_ln_b, w_conv_o, w_out, ln1_g, ln1_b, w_ff_gate, w_ff_up,
                  w_ff_down, ln2_g, ln2_b):
    w = w_in[l]
    o = np.cumsum([0, ATT_KV, ATT_KV, GLA_K, GLA_V, 2 * GLA_RANK, ATT_Q, GLA_K, GLA_V, 2 * D_MODEL,
                   3 * D_MODEL])
    k_w, v_w, gk_w, gv_w, glr_w, q_w, gq_w, r_w, glu_w, gates_w = [w[:, o[i]:o[i + 1]] for i in range(10)]
    w_main = jnp.concatenate([_deinterleave(q_w, N_HEADS), _deinterleave(k_w, N_KV_HEADS), gk_w, gv_w,
                              gq_w, r_w, glu_w, gates_w], axis=1).astype(BF16)
    a2 = gla_w_a2[l]
    zero = jnp.zeros((GLA_RANK, GLA_K), F32)
    w_a2 = jnp.concatenate([jnp.concatenate([a2[0], zero], axis=1),
                            jnp.concatenate([zero, a2[1]], axis=1)], axis=0).astype(BF16)
    vec = lambda a: a[l].reshape(1, -1)
    return {
        'w_in': w_main, 'w_vt': v_w.T.astype(BF16), 'w_glr': glr_w.astype(BF16), 'w_a2': w_a2,
        'b_a': gla_b_a[l].reshape(1, 2 * GLA_K),
        'q_norm': _deinterleave(q_norm[l].reshape(1, HEAD_DIM), 1),
        'k_norm': _deinterleave(k_norm[l].reshape(1, HEAD_DIM), 1),
        'w_att_o': w_att_o[l].astype(BF16), 'gla_norm': vec(gla_norm), 'w_gla_o': w_gla_o[l].astype(BF16),
        'conv_w_dw': jnp.pad(conv_w_dw[l][:, 0, :], ((0, 32 - CONV_W), (0, 0))),
        'conv_b_dw': vec(conv_b_dw), 'conv_ln_g': vec(conv_ln_g), 'conv_ln_b': vec(conv_ln_b),
        'w_conv_o': w_conv_o[l].astype(BF16), 'w_out': w_out[l].astype(BF16),
        'ln1_g': vec(ln1_g), 'ln1_b': vec(ln1_b),
        'w_ff_gate': w_ff_gate[l].astype(BF16), 'w_ff_up': w_ff_up[l].astype(BF16),
        'w_ff_down': w_ff_down[l].astype(BF16), 'ln2_g': vec(ln2_g), 'ln2_b': vec(ln2_b),
    }


def _key_chunk(n_keys):
    return max(t for t in (256, 512, 768, 1024) if n_keys % t == 0)


def kernel(x, c, ctx, c_ctx, w_ada, b_ada, w_in, q_norm, k_norm, w_att_o, gla_w_a2, gla_b_a, gla_norm,
           w_gla_o, conv_w_dw, conv_b_dw, conv_ln_g, conv_ln_b, w_conv_o, w_out, ln1_g, ln1_b,
           w_ff_gate, w_ff_up, w_ff_down, ln2_g, ln2_b):
    B, S, D = x.shape
    LC = ctx.shape[1]
    tm_lat = min(256, S)
    tm_ctx = min(256, LC)
    cos, sin = _rope_tables(S)
    tri_sizes = sorted({tm_lat, tm_ctx})
    tabs_lat = {'cos': cos, 'sin': sin, 'tri': {t: _tri_blocks(t) for t in tri_sizes}}
    tabs_ctx = {'cos': cos[:LC], 'sin': sin[:LC], 'tri': tabs_lat['tri']}
    idx = np.arange(GLA_C)
    tri2 = jnp.asarray(np.stack([idx[None, :] <= idx[:, None], idx[None, :] >= idx[:, None]]), F32)

    cvec = jnp.concatenate([c, c_ctx[None, :], jnp.zeros((8 - B - 1, D), F32)], axis=0)
    ada = _adaln(cvec, w_ada, b_ada)
    zero_state = jnp.zeros((2, B, GLA_HEADS, GLA_DK, GLA_DV), F32)

    xc = ctx
    for l in range(DEPTH):
        p = _layer_params(l, w_in, q_norm, k_norm, w_att_o, gla_w_a2, gla_b_a, gla_norm, w_gla_o,
                          conv_w_dw, conv_b_dw, conv_ln_g, conv_ln_b, w_conv_o, w_out, ln1_g, ln1_b,
                          w_ff_gate, w_ff_up, w_ff_down, ln2_g, ln2_b)
        mod = ada[l, :B].reshape(B, 6, D)
        mod_c = jnp.broadcast_to(ada[l, B].reshape(1, 6, D), (B, 6, D))
        last = l == DEPTH - 1

        qc, kc, vc, gkc, gvc, gqc, rc, yc, gatesc, bcc = _inproj(xc, mod_c, tabs_ctx, p, rope=False,
                                                                 tm=tm_ctx)
        ogf_c, ogb_c, s_ctx = _gla(gqc, gkc, gvc, bcc, zero_state, tri2, tb=tm_ctx)
        if not last:
            o_att_c = _attention(qc, [(kc, vc)], tq=min(128, LC), tk=_key_chunk(LC))
            xc = _merge(xc, mod_c, o_att_c, ogf_c, ogb_c, rc, yc, gatesc, p, tm=tm_ctx)
            xc = _ffn(xc, mod_c, p, tm=tm_ctx)

        q, k, v, gk, gv, gq, r, y, gates, bc = _inproj(x, mod, tabs_lat, p, rope=True, tm=tm_lat)
        o_att = _attention(q, [(k, v), (kc, vc)], tq=min(256, S), tk=_key_chunk(S + LC))
        ogf, ogb, _ = _gla(gq, gk, gv, bc, s_ctx, tri2, tb=min(512, S))
        x = _merge(x, mod, o_att, ogf, ogb, r, y, gates, p, tm=tm_lat)
        x = _ffn(x, mod, p, tm=tm_lat)
    return x
```

```python
import functools

import numpy as np
import jax
import jax.numpy as jnp
from jax import lax
from jax.experimental import pallas as pl
from jax.experimental.pallas import tpu as pltpu

F32 = jnp.float32
BF16 = jnp.bfloat16

D_MODEL = 1024
DEPTH = 2
GRID_W = 64
N_HEADS = 8
N_KV_HEADS = 2
HEAD_DIM = 128
Q_PER_KV = N_HEADS // N_KV_HEADS
ROPE_THETA = 10000.0
GLA_HEADS = 4
GLA_DK = 128
GLA_DV = 256
GLA_RANK = 16
GLA_TAU = 16.0
GLA_C = 128
CONV_W = 31
CONV_PAD = CONV_W // 2
HALO = 16
SUBLANES = 8
D_FF = 2816
ALPHA = (2.0 * DEPTH) ** 0.25
NORM_EPS = 1e-6
ATT_Q = N_HEADS * HEAD_DIM
ATT_KV = N_KV_HEADS * HEAD_DIM
V_ROWS = HEAD_DIM + 16
GLA_K = GLA_HEADS * GLA_DK
GLA_V = GLA_HEADS * GLA_DV
DECAY_SAFE = 160.0
LOG2E = 1.4426950408889634
L_MIN = 1e-20

C_Q = 0
C_K = C_Q + ATT_Q
C_GK = C_K + ATT_KV
C_GV = C_GK + GLA_K
C_GQ = C_GV + GLA_V
C_R = C_GQ + GLA_K
C_GLU_A = C_R + GLA_V
C_GLU_G = C_GLU_A + D_MODEL
C_GATES = C_GLU_G + D_MODEL
C_END = C_GATES + 3 * D_MODEL

VMEM_LIMIT = 56 * 1024 * 1024


def _cparams(sem):
    return pltpu.CompilerParams(dimension_semantics=sem, vmem_limit_bytes=VMEM_LIMIT)


def _resident(shape):
    nd = len(shape)
    return pl.BlockSpec(shape, lambda *_: (0,) * nd, pipeline_mode=pl.Buffered(1))


def _ln(x):
    mu = jnp.mean(x, axis=-1, keepdims=True)
    xc = x - mu
    var = jnp.mean(xc * xc, axis=-1, keepdims=True)
    return xc * lax.rsqrt(var + NORM_EPS)


def _silu(x):
    return x * jax.nn.sigmoid(x)


def _dot(a, b):
    return jnp.dot(a, b, preferred_element_type=F32)


def _dot_nt(a, b):
    return lax.dot_general(a, b, (((1,), (1,)), ((), ())), preferred_element_type=F32)


def _dot_tn(a, b):
    return lax.dot_general(a, b, (((0,), (0,)), ((), ())), preferred_element_type=F32)


def _adaln_kernel(c_ref, w_ref, b_ref, o_ref):
    s = _silu(c_ref[...])
    o_ref[0] = jnp.dot(s, w_ref[0], preferred_element_type=F32,
                       precision=lax.Precision.HIGHEST) + b_ref[0]


def _adaln(cvec, w_ada, b_ada):
    L, D, N6 = w_ada.shape
    tn = 1536
    return pl.pallas_call(
        _adaln_kernel,
        out_shape=jax.ShapeDtypeStruct((L, 8, N6), F32),
        grid=(L, N6 // tn),
        in_specs=[pl.BlockSpec((8, D), lambda l, n: (0, 0)),
                  pl.BlockSpec((1, D, tn), lambda l, n: (l, 0, n)),
                  pl.BlockSpec((1, 1, tn), lambda l, n: (l, 0, n))],
        out_specs=pl.BlockSpec((1, 8, tn), lambda l, n: (l, 0, n)),
        compiler_params=_cparams(("arbitrary", "arbitrary")),
        name="adaln",
    )(cvec, w_ada, b_ada.reshape(L, 1, N6))


def _inproj_kernel(x_ref, mod_ref, cos_ref, sin_ref, qn_ref, kn_ref, ba_ref, w_ref, wvt_ref, wglr_ref,
                   w2_ref, tril_ref, triu_ref,
                   q_o, k_o, vt_o, gk_o, gv_o, gq_o, r_o, y_o, gates_o, bc_o, h_scr, *, rope):
    shift = mod_ref[0, 0:1, :]
    scale = mod_ref[0, 1:2, :]
    h_scr[...] = (_ln(x_ref[0]) * (1.0 + scale) + shift).astype(BF16)
    tm = h_scr.shape[0]

    def mm(c0, width):
        return _dot(h_scr[...], w_ref[:, c0:c0 + width])

    def norm_rope(xh, gain):
        y = xh * lax.rsqrt(jnp.mean(xh * xh, axis=-1, keepdims=True) + NORM_EPS) * gain
        if rope:
            y = y * cos_ref[...] + pltpu.roll(y, HEAD_DIM // 2, 1) * sin_ref[...]
        return y

    for c in range(ATT_Q // 512):
        acc = mm(C_Q + c * 512, 512)
        for g in range(4):
            y = norm_rope(acc[:, g * 128:(g + 1) * 128], qn_ref[...]) * (LOG2E * HEAD_DIM ** -0.5)
            q_o[0, :, c * 512 + g * 128:c * 512 + (g + 1) * 128] = y.astype(BF16)
    acc = mm(C_K, ATT_KV)
    for g in range(N_KV_HEADS):
        y = norm_rope(acc[:, g * 128:(g + 1) * 128], kn_ref[...])
        k_o[0, :, g * 128:(g + 1) * 128] = y.astype(BF16)
    vt = _dot_nt(wvt_ref[...], h_scr[...])
    for g in range(N_KV_HEADS):
        vt_o[0, g * V_ROWS:g * V_ROWS + HEAD_DIM, :] = vt[g * 128:(g + 1) * 128].astype(BF16)
        vt_o[0, g * V_ROWS + HEAD_DIM:(g + 1) * V_ROWS, :] = jnp.ones((V_ROWS - HEAD_DIM, tm), BF16)

    gk_o[0] = mm(C_GK, GLA_K).astype(BF16)
    for c in range(GLA_V // 512):
        gv_o[0, :, c * 512:(c + 1) * 512] = mm(C_GV + c * 512, 512).astype(BF16)
    gq_o[0] = (mm(C_GQ, GLA_K) * (GLA_DK ** -0.5)).astype(BF16)
    for c in range(GLA_V // 512):
        r_o[0, :, c * 512:(c + 1) * 512] = _silu(mm(C_R + c * 512, 512)).astype(BF16)
    for c in range(D_MODEL // 512):
        a = mm(C_GLU_A + c * 512, 512)
        g = mm(C_GLU_G + c * 512, 512)
        y_o[0, :, c * 512:(c + 1) * 512] = (a * jax.nn.sigmoid(g)).astype(BF16)
    for c in range(3 * D_MODEL // 512):
        gates_o[0, :, c * 512:(c + 1) * 512] = jax.nn.sigmoid(mm(C_GATES + c * 512, 512)).astype(BF16)

    glr = _dot(h_scr[...], wglr_ref[...])
    z = _dot(glr.astype(BF16), w2_ref[...]) + ba_ref[...]
    lg = (jnp.minimum(z, 0.0) - jnp.log(1.0 + jnp.exp(-jnp.abs(z)))) * (1.0 / GLA_TAU)
    for dr, tri_ref in ((0, tril_ref), (1, triu_ref)):
        g = lg[:, dr * GLA_K:(dr + 1) * GLA_K]
        g1 = g.astype(BF16)
        r1 = g - g1.astype(F32)
        g2 = r1.astype(BF16)
        g3 = (r1 - g2.astype(F32)).astype(BF16)
        tri = tri_ref[...]
        bc_o[dr, 0] = _dot(tri, g1) + _dot(tri, g2) + _dot(tri, g3)


def _inproj(x, mod, tabs, p, *, rope, tm):
    B, N, D = x.shape
    nt = N // tm
    row = lambda w: pl.BlockSpec((1, tm, w), lambda b, i: (b, i, 0))
    widths = (ATT_Q, ATT_KV, GLA_K, GLA_V, GLA_K, GLA_V, D_MODEL, 3 * D_MODEL)
    out_shapes = [jax.ShapeDtypeStruct((B, N, w), BF16) for w in widths]
    out_specs = [row(w) for w in widths]
    out_shapes.insert(2, jax.ShapeDtypeStruct((B, N_KV_HEADS * V_ROWS, N), BF16))
    out_specs.insert(2, pl.BlockSpec((1, N_KV_HEADS * V_ROWS, tm), lambda b, i: (b, 0, i)))
    out_shapes.append(jax.ShapeDtypeStruct((2, B, N, GLA_K), F32))
    out_specs.append(pl.BlockSpec((2, 1, tm, GLA_K), lambda b, i: (0, b, i, 0)))
    tril, triu = tabs['tri'][tm]
    return pl.pallas_call(
        functools.partial(_inproj_kernel, rope=rope),
        out_shape=out_shapes,
        grid=(B, nt),
        in_specs=[row(D),
                  pl.BlockSpec((1, 6, D), lambda b, i: (b, 0, 0)),
                  pl.BlockSpec((tm, HEAD_DIM), lambda b, i: (i, 0)),
                  pl.BlockSpec((tm, HEAD_DIM), lambda b, i: (i, 0)),
                  _resident((1, HEAD_DIM)), _resident((1, HEAD_DIM)), _resident((1, 2 * GLA_K)),
                  _resident((D, C_END)), _resident((ATT_KV, D)), _resident((D, 2 * GLA_RANK)),
                  _resident((2 * GLA_RANK, 2 * GLA_K)),
                  _resident((tm, tm)), _resident((tm, tm))],
        out_specs=out_specs,
        scratch_shapes=[pltpu.VMEM((tm, D), BF16)],
        compiler_params=_cparams(("arbitrary", "arbitrary")),
        name="in_proj_rope" if rope else "in_proj",
    )(x, mod, tabs['cos'], tabs['sin'], p['q_norm'], p['k_norm'], p['b_a'], p['w_in'], p['w_vt'],
      p['w_glr'], p['w_a2'], tril, triu)


def _dwconv(y_ref, yp_ref, yn_ref, w_ref, o_ref, ybuf, ysh, first, last):
    n = y_ref.shape[1]
    span = ysh.shape[1]

    def fill():
        ybuf[0:HALO, :] = jnp.where(first, 0.0, yp_ref[0].astype(F32))
        ybuf[HALO:HALO + n, :] = y_ref[0].astype(F32)
        ybuf[HALO + n:, :] = jnp.where(last, 0.0, yn_ref[0].astype(F32))

    def lane_block(cb, start_zero):
        lanes = slice(cb * 128, (cb + 1) * 128)
        for s in range(1, SUBLANES):
            ysh[s - 1, :, lanes] = ybuf[s:s + span, lanes]
        acc = jnp.broadcast_to(start_zero, (n, 128))
        for t in range(CONV_W):
            a, s = divmod(HALO - CONV_PAD + t, SUBLANES)
            rows = slice(a * SUBLANES, a * SUBLANES + n)
            src = ybuf[rows, lanes] if s == 0 else ysh[s - 1, rows, lanes]
            acc = acc + src * w_ref[t:t + 1, lanes]
        o_ref[0, :, lanes] = acc
        return _zero_after(functools.reduce(jnp.maximum, [acc[r:r + SUBLANES] for r in range(0, n, SUBLANES)]))

    return fill, [functools.partial(lane_block, cb) for cb in range(D_MODEL // 128)]


def _zero_after(x):
    return ((pltpu.bitcast(x, jnp.uint32) >> 16) >> 16).astype(F32)


def _attn_kernel(*refs, src_len, tq, tk):
    n_src = len(src_len)
    q_ref = refs[0]
    src_refs = refs[1:1 + 2 * n_src]
    y_ref, yp_ref, yn_ref, wdw_ref, o_ref, conv_ref = refs[1 + 2 * n_src:7 + 2 * n_src]
    k_scr, vt_scr, kmax_scr, acc_scr, m_scr, ybuf, ysh = refs[7 + 2 * n_src:]
    M = Q_PER_KV * tq
    n_chunk = sum(src_len) // tk
    conv_blk = pl.program_id(1) * pl.num_programs(2) + pl.program_id(2)
    n_conv_blk = pl.num_programs(1) * pl.num_programs(2)

    def rolled(fn):
        def body(i, carry):
            rows = pl.ds(pl.multiple_of(i * tk, tk), tk)
            fn(k_scr[rows, :], vt_scr[:, rows])
            return carry
        lax.fori_loop(0, n_chunk, body, 0)

    @pl.when(pl.program_id(2) == 0)
    def _():
        off = 0
        for s_i, n in enumerate(src_len):
            k_scr[off:off + n, :] = src_refs[2 * s_i][0]
            vt_scr[:, off:off + n] = src_refs[2 * s_i + 1][0]
            off += n
        kmax_scr[...] = jnp.zeros(kmax_scr.shape, F32)

        def knorm(k, vt):
            kf = k.astype(F32)
            n2 = jnp.max(jnp.sum(kf * kf, axis=1, keepdims=True), axis=0, keepdims=True)
            kmax_scr[...] = jnp.maximum(kmax_scr[...], n2)
        rolled(knorm)

    q4 = jnp.concatenate([q_ref[0, :, g * 128:(g + 1) * 128] for g in range(Q_PER_KV)], axis=0)
    qf = q4.astype(F32)
    qn2 = _dot_nt(jnp.ones((SUBLANES, HEAD_DIM), BF16), (qf * qf).astype(BF16))[0:1, :]
    shift = jnp.sqrt(qn2 * kmax_scr[0:1, 0:1])

    conv_fill, conv_pieces = _dwconv(y_ref, yp_ref, yn_ref, wdw_ref, conv_ref, ybuf, ysh,
                                     conv_blk == 0, conv_blk == n_conv_blk - 1)
    conv_fill()
    per_chunk = -(-len(conv_pieces) // max(n_chunk - 2, 1))

    acc = jnp.zeros((V_ROWS, M), F32)
    piece_zeros = {}
    for c in range(n_chunk):
        shift_c = shift
        for zero in piece_zeros.pop(c, []):
            shift_c = jnp.concatenate([shift_c[:, :128] + zero[0:1, :], shift_c[:, 128:]], axis=1)
        rows = slice(c * tk, (c + 1) * tk)
        st = _dot_nt(k_scr[rows, :], q4)
        acc = acc + _dot(vt_scr[:, rows], jnp.exp2(st - shift_c).astype(BF16))
        for piece in conv_pieces[c * per_chunk:(c + 1) * per_chunk]:
            zero = piece(_zero_after(acc[0:1, 0:128]))
            if c + 3 < n_chunk:
                piece_zeros.setdefault(c + 3, []).append(zero)
    acc_scr[...] = acc

    def write_out():
        a = acc_scr[...]
        ot = a[:HEAD_DIM, :] / a[HEAD_DIM:HEAD_DIM + 1, :]
        for g in range(Q_PER_KV):
            o_ref[0, :, g * 128:(g + 1) * 128] = jnp.transpose(ot[:, g * tq:(g + 1) * tq]).astype(BF16)
    write_out()

    l_min = jnp.min(acc_scr[HEAD_DIM:HEAD_DIM + 1, :])

    @pl.when(jnp.logical_not(l_min >= L_MIN))
    def _():
        m_scr[...] = jnp.full(m_scr.shape, -jnp.inf, F32)
        acc_scr[...] = jnp.zeros(acc_scr.shape, F32)

        def online(k, vt):
            st = _dot_nt(k, q4)
            m_prev = m_scr[0:1, :]
            m_new = jnp.maximum(m_prev, jnp.max(st, axis=0, keepdims=True))
            alpha = jnp.exp2(m_prev - m_new)
            acc_scr[...] = alpha * acc_scr[...] + _dot(vt, jnp.exp2(st - m_new).astype(BF16))
            m_scr[...] = jnp.broadcast_to(m_new, m_scr.shape)
        rolled(online)
        write_out()


def _attention(q, srcs, y, w_dw, *, tq, tk):
    B, Nq, D = y.shape
    src_len = tuple(k.shape[1] for k, _ in srcs)
    lk = sum(src_len)
    assert lk % tk == 0
    M = Q_PER_KV * tq
    nq = Nq // tq
    cr = tq // N_KV_HEADS
    hb = cr // HALO
    n_halo = Nq // HALO
    conv_blk = lambda h, i: h * nq + i
    in_specs = [pl.BlockSpec((1, tq, Q_PER_KV * HEAD_DIM), lambda b, h, i: (b, i, h))]
    args = [q]
    for (k, vt), n in zip(srcs, src_len):
        in_specs += [pl.BlockSpec((1, n, HEAD_DIM), lambda b, h, i: (b, 0, h)),
                     pl.BlockSpec((1, V_ROWS, n), lambda b, h, i: (b, h, 0))]
        args += [k, vt]
    in_specs += [pl.BlockSpec((1, cr, D), lambda b, h, i: (b, conv_blk(h, i), 0)),
                 pl.BlockSpec((1, HALO, D), lambda b, h, i: (b, jnp.maximum(conv_blk(h, i) * hb - 1, 0), 0)),
                 pl.BlockSpec((1, HALO, D),
                              lambda b, h, i: (b, jnp.minimum((conv_blk(h, i) + 1) * hb, n_halo - 1), 0)),
                 _resident((32, D))]
    args += [y, y, y, w_dw]
    return pl.pallas_call(
        functools.partial(_attn_kernel, src_len=src_len, tq=tq, tk=tk),
        out_shape=[jax.ShapeDtypeStruct((B, Nq, ATT_Q), BF16), jax.ShapeDtypeStruct((B, Nq, D), F32)],
        grid=(B, N_KV_HEADS, nq),
        in_specs=in_specs,
        out_specs=[pl.BlockSpec((1, tq, Q_PER_KV * HEAD_DIM), lambda b, h, i: (b, i, h)),
                   pl.BlockSpec((1, cr, D), lambda b, h, i: (b, conv_blk(h, i), 0))],
        scratch_shapes=[pltpu.VMEM((lk, HEAD_DIM), BF16), pltpu.VMEM((V_ROWS, lk), BF16),
                        pltpu.VMEM((SUBLANES, 128), F32), pltpu.VMEM((V_ROWS, M), F32),
                        pltpu.VMEM((SUBLANES, M), F32),
                        pltpu.VMEM((cr + 2 * HALO, D), F32),
                        pltpu.VMEM((SUBLANES - 1, cr + 2 * HALO - SUBLANES, D), F32)],
        compiler_params=_cparams(("arbitrary", "arbitrary", "arbitrary")),
        name="attn%d" % len(srcs),
    )(*args)


def _gla_kernel(qf_ref, kf_ref, vf_ref, bf_ref, qb_ref, kb_ref, vb_ref, bb_ref, tri_ref, s0_ref,
                of_ref, ob_ref, send_ref, s_scr, *, n_chunk):
    j = pl.program_id(1)

    @pl.when(j == 0)
    def _():
        s_scr[...] = s0_ref[:, 0]

    dirs = ((qf_ref, kf_ref, vf_ref, bf_ref, of_ref), (qb_ref, kb_ref, vb_ref, bb_ref, ob_ref))

    def intra_exact(q, k, b):
        row_id = lax.broadcasted_iota(jnp.int32, (GLA_C, GLA_DK), 0)
        col_id = lax.broadcasted_iota(jnp.int32, (GLA_C, GLA_C), 1)

        def one(jrow, a):
            pick = row_id == jrow
            k_j = jnp.sum(jnp.where(pick, k, 0.0), axis=0, keepdims=True)
            b_j = jnp.sum(jnp.where(pick, b, 0.0), axis=0, keepdims=True)
            col = jnp.sum(q * k_j * jnp.exp(jnp.minimum(b - b_j, 0.0)), axis=1, keepdims=True)
            return jnp.where(col_id == jrow, col, a)
        return lax.fori_loop(0, GLA_C, one, jnp.zeros((GLA_C, GLA_C), F32))

    def make_body(exact):
        def body(i, carry):
            for d, (q_ref, k_ref, v_ref, b_ref, o_ref) in enumerate(dirs):
                c = i if d == 0 else n_chunk - 1 - i
                rows = pl.ds(pl.multiple_of(c * GLA_C, GLA_C), GLA_C)
                tri = tri_ref[d]
                for h in range(GLA_HEADS):
                    kc = slice(h * GLA_DK, (h + 1) * GLA_DK)
                    vc = slice(h * GLA_DV, (h + 1) * GLA_DV)
                    b = b_ref[0, 0, rows, kc]
                    q = q_ref[0, rows, kc].astype(F32)
                    k = k_ref[0, rows, kc].astype(F32)
                    v = v_ref[0, rows, vc]
                    b_end = jnp.min(b, axis=0, keepdims=True)
                    qe = (q * jnp.exp(b)).astype(BF16)
                    kh = k * jnp.exp(b_end - b)
                    if exact:
                        a = intra_exact(q, k, b)
                    else:
                        half = 0.5 * b_end
                        a = _dot_nt((q * jnp.exp(b - half)).astype(BF16),
                                    (k * jnp.exp(half - b)).astype(BF16))
                    a = a * tri
                    av = _dot(jnp.concatenate([a.astype(BF16), jnp.transpose(kh).astype(BF16)], axis=0), v)
                    s = s_scr[d, h]
                    o_ref[0, rows, vc] = (_dot(qe, s.astype(BF16)) + av[:GLA_C]).astype(BF16)
                    e_col = jnp.transpose(jnp.broadcast_to(jnp.exp(b_end), (GLA_DK, GLA_DK)))
                    s_scr[d, h] = jnp.concatenate([e_col, e_col], axis=1) * s + av[GLA_C:]
            return carry
        return body

    safe = jnp.minimum(jnp.min(bf_ref[...]), jnp.min(bb_ref[...])) >= -DECAY_SAFE

    @pl.when(safe)
    def _():
        lax.fori_loop(0, n_chunk, make_body(False), 0, unroll=2)

    @pl.when(jnp.logical_not(safe))
    def _():
        lax.fori_loop(0, n_chunk, make_body(True), 0)

    @pl.when(j == pl.num_programs(1) - 1)
    def _():
        send_ref[:, 0] = s_scr[...]


def _gla(gq, gk, gv, bc, s0, tri2, *, tb):
    B, N, _ = gq.shape
    nb = N // tb
    fwd = lambda w: pl.BlockSpec((1, tb, w), lambda b, j: (b, j, 0))
    bwd = lambda w: pl.BlockSpec((1, tb, w), lambda b, j: (b, nb - 1 - j, 0))
    state = pl.BlockSpec((2, 1, GLA_HEADS, GLA_DK, GLA_DV), lambda b, j: (0, b, 0, 0, 0))
    return pl.pallas_call(
        functools.partial(_gla_kernel, n_chunk=tb // GLA_C),
        out_shape=[jax.ShapeDtypeStruct((B, N, GLA_V), BF16), jax.ShapeDtypeStruct((B, N, GLA_V), BF16),
                   jax.ShapeDtypeStruct((2, B, GLA_HEADS, GLA_DK, GLA_DV), F32)],
        grid=(B, nb),
        in_specs=[fwd(GLA_K), fwd(GLA_K), fwd(GLA_V),
                  pl.BlockSpec((1, 1, tb, GLA_K), lambda b, j: (0, b, j, 0)),
                  bwd(GLA_K), bwd(GLA_K), bwd(GLA_V),
                  pl.BlockSpec((1, 1, tb, GLA_K), lambda b, j: (1, b, nb - 1 - j, 0)),
                  _resident((2, GLA_C, GLA_C)), state],
        out_specs=[fwd(GLA_V), bwd(GLA_V), state],
        scratch_shapes=[pltpu.VMEM((2, GLA_HEADS, GLA_DK, GLA_DV), F32)],
        compiler_params=_cparams(("arbitrary", "arbitrary")),
        name="gla",
    )(gq, gk, gv, bc, gq, gk, gv, bc, tri2, s0)


def _merge_kernel(x_ref, mod_ref, oatt_ref, ogf_ref, ogb_ref, r_ref, conv_ref, gates_ref,
                  gn_ref, bdw_ref, clg_ref, clb_ref, watt_ref, wgla_ref, wconv_ref, wout_ref,
                  lng_ref, lnb_ref, xo_ref, *, sub):
    for r0 in range(0, x_ref.shape[1], sub):
        rows = slice(r0, r0 + sub)
        yc = _ln(conv_ref[0, rows, :] + bdw_ref[...]) * clg_ref[...] + clb_ref[...]
        y_conv = _dot(_silu(yc).astype(BF16), wconv_ref[...])

        y_att = _dot(oatt_ref[0, rows, :], watt_ref[...])

        parts = []
        for h in range(GLA_HEADS):
            vc = slice(h * GLA_DV, (h + 1) * GLA_DV)
            o = ogf_ref[0, rows, vc].astype(F32) + ogb_ref[0, rows, vc].astype(F32)
            o = o * lax.rsqrt(jnp.mean(o * o, axis=-1, keepdims=True) + NORM_EPS) * gn_ref[...]
            parts.append((o * r_ref[0, rows, vc].astype(F32)).astype(BF16))
        y_gla = _dot(jnp.concatenate(parts, axis=1), wgla_ref[...])

        g_att = gates_ref[0, rows, 0:D_MODEL].astype(F32)
        g_gla = gates_ref[0, rows, D_MODEL:2 * D_MODEL].astype(F32)
        g_conv = gates_ref[0, rows, 2 * D_MODEL:].astype(F32)
        mix = g_att * y_att + g_gla * y_gla + g_conv * y_conv
        f = _dot(mix.astype(BF16), wout_ref[...])
        gate = mod_ref[0, 2:3, :]
        xo_ref[0, rows, :] = _ln(ALPHA * x_ref[0, rows, :] + gate * f) * lng_ref[...] + lnb_ref[...]


def _merge(x, mod, o_att, o_gla_f, o_gla_b, r, conv, gates, p, *, tm):
    B, N, D = x.shape
    row = lambda w: pl.BlockSpec((1, tm, w), lambda b, i: (b, i, 0))
    vec = _resident((1, D))
    return pl.pallas_call(
        functools.partial(_merge_kernel, sub=min(256, tm)),
        out_shape=jax.ShapeDtypeStruct((B, N, D), F32),
        grid=(B, N // tm),
        in_specs=[row(D),
                  pl.BlockSpec((1, 6, D), lambda b, i: (b, 0, 0)),
                  row(ATT_Q), row(GLA_V), row(GLA_V), row(GLA_V), row(D), row(3 * D),
                  _resident((1, GLA_DV)), vec, vec, vec,
                  _resident((ATT_Q, D)), _resident((GLA_V, D)), _resident((D, D)), _resident((D, D)),
                  vec, vec],
        out_specs=row(D),
        compiler_params=_cparams(("arbitrary", "arbitrary")),
        name="merge",
    )(x, mod, o_att, o_gla_f, o_gla_b, r, conv, gates,
      p['gla_norm'], p['conv_b_dw'], p['conv_ln_g'], p['conv_ln_b'],
      p['w_att_o'], p['w_gla_o'], p['w_conv_o'], p['w_out'], p['ln1_g'], p['ln1_b'])


def _ffn_kernel(x_ref, mod_ref, wg_ref, wu_ref, wd_ref, lng_ref, lnb_ref, xo_ref, h_scr, a_scr, *, tf, sub):
    for r0 in range(0, x_ref.shape[1], sub):
        rows = slice(r0, r0 + sub)
        x = x_ref[0, rows, :]
        h_scr[rows, :] = (_ln(x) * (1.0 + mod_ref[0, 4:5, :]) + mod_ref[0, 3:4, :]).astype(BF16)
        for c0 in range(0, D_FF, tf):
            cols = slice(c0, min(c0 + tf, D_FF))
            g = _dot(h_scr[rows, :], wg_ref[:, cols])
            u = _dot(h_scr[rows, :], wu_ref[:, cols])
            a_scr[rows, cols] = (_silu(g) * u).astype(BF16)
        f = _dot(a_scr[rows, :], wd_ref[...])
        xo_ref[0, rows, :] = _ln(ALPHA * x + mod_ref[0, 5:6, :] * f) * lng_ref[...] + lnb_ref[...]


def _ffn(x, mod, p, *, tm):
    B, N, D = x.shape
    row = pl.BlockSpec((1, tm, D), lambda b, i: (b, i, 0))
    return pl.pallas_call(
        functools.partial(_ffn_kernel, tf=1024, sub=min(256, tm)),
        out_shape=jax.ShapeDtypeStruct((B, N, D), F32),
        grid=(B, N // tm),
        in_specs=[row, pl.BlockSpec((1, 6, D), lambda b, i: (b, 0, 0)),
                  _resident((D, D_FF)), _resident((D, D_FF)), _resident((D_FF, D)),
                  _resident((1, D)), _resident((1, D))],
        out_specs=row,
        scratch_shapes=[pltpu.VMEM((tm, D), BF16), pltpu.VMEM((tm, D_FF), BF16)],
        compiler_params=_cparams(("arbitrary", "arbitrary")),
        name="ffn",
    )(x, mod, p['w_ff_gate'], p['w_ff_up'], p['w_ff_down'], p['ln2_g'], p['ln2_b'])


def _deinterleave(w, n_heads):
    d = w.shape[0]
    return w.reshape(d, n_heads, HEAD_DIM // 2, 2).transpose(0, 1, 3, 2).reshape(d, n_heads * HEAD_DIM)


def _tri_blocks(tm):
    idx = np.arange(tm)
    same = (idx[:, None] // GLA_C) == (idx[None, :] // GLA_C)
    tril = same & (idx[None, :] <= idx[:, None])
    triu = same & (idx[None, :] >= idx[:, None])
    return jnp.asarray(tril, BF16), jnp.asarray(triu, BF16)


def _rope_tables(S):
    t = np.arange(S)
    row = (t // GRID_W).astype(np.float32)
    col = (t % GRID_W).astype(np.float32)
    half = HEAD_DIM // 2
    inv = (ROPE_THETA ** (-np.arange(0, half, 2, dtype=np.float32) / half)).astype(np.float32)
    ang = np.concatenate([row[:, None] * inv, col[:, None] * inv], axis=-1).astype(np.float32)
    cos, sin = np.cos(ang), np.sin(ang)
    return (jnp.asarray(np.concatenate([cos, cos], axis=-1), F32),
            jnp.asarray(np.concatenate([-sin, sin], axis=-1), F32))


def _layer_params(l, w_in, q_norm, k_norm, w_att_o, gla_w_a2, gla_b_a, gla_norm, w_gla_o, conv_w_dw,
                  conv_b_dw, conv_ln_g, conv_ln_b, w_conv_o, w_out, ln1_g, ln1_b, w_ff_gate, w_ff_up,
                  w_ff_down, ln2_g, ln2_b):
    w = w_in[l]
    o = np.cumsum([0, ATT_KV, ATT_KV, GLA_K, GLA_V, 2 * GLA_RANK, ATT_Q, GLA_K, GLA_V, 2 * D_MODEL,
                   3 * D_MODEL])
    k_w, v_w, gk_w, gv_w, glr_w, q_w, gq_w, r_w, glu_w, gates_w = [w[:, o[i]:o[i + 1]] for i in range(10)]
    w_main = jnp.concatenate([_deinterleave(q_w, N_HEADS), _deinterleave(k_w, N_KV_HEADS), gk_w, gv_w,
                              gq_w, r_w, glu_w, gates_w], axis=1).astype(BF16)
    a2 = gla_w_a2[l]
    zero = jnp.zeros((GLA_RANK, GLA_K), F32)
    w_a2 = jnp.concatenate([jnp.concatenate([a2[0], zero], axis=1),
                            jnp.concatenate([zero, a2[1]], axis=1)], axis=0).astype(BF16)
    vec = lambda a: a[l].reshape(1, -1)
    return {
        'w_in': w_main, 'w_vt': v_w.T.astype(BF16), 'w_glr': glr_w.astype(BF16), 'w_a2': w_a2,
        'b_a': gla_b_a[l].reshape(1, 2 * GLA_K),
        'q_norm': _deinterleave(q_norm[l].reshape(1, HEAD_DIM), 1),
        'k_norm': _deinterleave(k_norm[l].reshape(1, HEAD_DIM), 1),
        'w_att_o': w_att_o[l].astype(BF16), 'gla_norm': vec(gla_norm), 'w_gla_o': w_gla_o[l].astype(BF16),
        'conv_w_dw': jnp.pad(conv_w_dw[l][:, 0, :], ((0, 32 - CONV_W), (0, 0))),
        'conv_b_dw': vec(conv_b_dw), 'conv_ln_g': vec(conv_ln_g), 'conv_ln_b': vec(conv_ln_b),
        'w_conv_o': w_conv_o[l].astype(BF16), 'w_out': w_out[l].astype(BF16),
        'ln1_g': vec(ln1_g), 'ln1_b': vec(ln1_b),
        'w_ff_gate': w_ff_gate[l].astype(BF16), 'w_ff_up': w_ff_up[l].astype(BF16),
        'w_ff_down': w_ff_down[l].astype(BF16), 'ln2_g': vec(ln2_g), 'ln2_b': vec(ln2_b),
    }


def _key_chunk(n_keys):
    return max(t for t in (256, 512, 768, 1024) if n_keys % t == 0)


def kernel(x, c, ctx, c_ctx, w_ada, b_ada, w_in, q_norm, k_norm, w_att_o, gla_w_a2, gla_b_a, gla_norm,
           w_gla_o, conv_w_dw, conv_b_dw, conv_ln_g, conv_ln_b, w_conv_o, w_out, ln1_g, ln1_b,
           w_ff_gate, w_ff_up, w_ff_down, ln2_g, ln2_b):
    B, S, D = x.shape
    LC = ctx.shape[1]
    tm_lat = min(256, S)
    tm_ctx = min(256, LC)
    cos, sin = _rope_tables(S)
    tri_sizes = sorted({tm_lat, tm_ctx})
    tabs_lat = {'cos': cos, 'sin': sin, 'tri': {t: _tri_blocks(t) for t in tri_sizes}}
    tabs_ctx = {'cos': cos[:LC], 'sin': sin[:LC], 'tri': tabs_lat['tri']}
    idx = np.arange(GLA_C)
    tri2 = jnp.asarray(np.stack([idx[None, :] <= idx[:, None], idx[None, :] >= idx[:, None]]), F32)

    cvec = jnp.concatenate([c, c_ctx[None, :], jnp.zeros((8 - B - 1, D), F32)], axis=0)
    ada = _adaln(cvec, w_ada, b_ada)
    zero_state = jnp.zeros((2, B, GLA_HEADS, GLA_DK, GLA_DV), F32)

    xc = ctx
    for l in range(DEPTH):
        p = _layer_params(l, w_in, q_norm, k_norm, w_att_o, gla_w_a2, gla_b_a, gla_norm, w_gla_o,
                          conv_w_dw, conv_b_dw, conv_ln_g, conv_ln_b, w_conv_o, w_out, ln1_g, ln1_b,
                          w_ff_gate, w_ff_up, w_ff_down, ln2_g, ln2_b)
        mod = ada[l, :B].reshape(B, 6, D)
        mod_c = jnp.broadcast_to(ada[l, B].reshape(1, 6, D), (B, 6, D))
        last = l == DEPTH - 1

        qc, kc, vc, gkc, gvc, gqc, rc, yc, gatesc, bcc = _inproj(xc, mod_c, tabs_ctx, p, rope=False,
                                                                 tm=tm_ctx)
        ogf_c, ogb_c, s_ctx = _gla(gqc, gkc, gvc, bcc, zero_state, tri2, tb=tm_ctx)
        if not last:
            o_att_c, conv_c = _attention(qc, [(kc, vc)], yc, p['conv_w_dw'], tq=min(128, LC),
                                         tk=_key_chunk(LC))
            xc = _merge(xc, mod_c, o_att_c, ogf_c, ogb_c, rc, conv_c, gatesc, p, tm=tm_ctx)
            xc = _ffn(xc, mod_c, p, tm=tm_ctx)

        q, k, v, gk, gv, gq, r, y, gates, bc = _inproj(x, mod, tabs_lat, p, rope=True, tm=tm_lat)
        o_att, conv = _attention(q, [(k, v), (kc, vc)], y, p['conv_w_dw'], tq=min(256, S),
                                 tk=_key_chunk(S + LC))
        ogf, ogb, _ = _gla(gq, gk, gv, bc, s_ctx, tri2, tb=min(512, S))
        x = _merge(x, mod, o_att, ogf, ogb, r, conv, gates, p, tm=min(512, S))
        x = _ffn(x, mod, p, tm=min(512, S))
    return x
```

```python
import functools

import numpy as np
import jax
import jax.numpy as jnp
from jax import lax
from jax.experimental import pallas as pl
from jax.experimental.pallas import tpu as pltpu

F32 = jnp.float32
BF16 = jnp.bfloat16

D_MODEL = 1024
DEPTH = 2
GRID_W = 64
N_HEADS = 8
N_KV_HEADS = 2
HEAD_DIM = 128
Q_PER_KV = N_HEADS // N_KV_HEADS
ROPE_THETA = 10000.0
GLA_HEADS = 4
GLA_DK = 128
GLA_DV = 256
GLA_RANK = 16
GLA_TAU = 16.0
GLA_C = 128
CONV_W = 31
CONV_PAD = CONV_W // 2
HALO = 16
SUBLANES = 8
D_FF = 2816
ALPHA = (2.0 * DEPTH) ** 0.25
NORM_EPS = 1e-6
ATT_Q = N_HEADS * HEAD_DIM
ATT_KV = N_KV_HEADS * HEAD_DIM
V_ROWS = HEAD_DIM + 16
GLA_K = GLA_HEADS * GLA_DK
GLA_V = GLA_HEADS * GLA_DV
DECAY_SAFE = 160.0
LOG2E = 1.4426950408889634
L_MIN = 1e-20

C_Q = 0
C_K = C_Q + ATT_Q
C_GK = C_K + ATT_KV
C_GV = C_GK + GLA_K
C_GQ = C_GV + GLA_V
C_R = C_GQ + GLA_K
C_GLU_A = C_R + GLA_V
C_GLU_G = C_GLU_A + D_MODEL
C_GATES = C_GLU_G + D_MODEL
C_END = C_GATES + 3 * D_MODEL

VMEM_LIMIT = 56 * 1024 * 1024


def _cparams(sem):
    return pltpu.CompilerParams(dimension_semantics=sem, vmem_limit_bytes=VMEM_LIMIT)


def _resident(shape):
    nd = len(shape)
    return pl.BlockSpec(shape, lambda *_: (0,) * nd, pipeline_mode=pl.Buffered(1))


def _layer_resident(shape, l):
    nd = len(shape)
    return pl.BlockSpec((None,) + tuple(shape), lambda *_: (l,) + (0,) * nd, pipeline_mode=pl.Buffered(1))


def _ln(x):
    mu = jnp.mean(x, axis=-1, keepdims=True)
    xc = x - mu
    var = jnp.mean(xc * xc, axis=-1, keepdims=True)
    return xc * lax.rsqrt(var + NORM_EPS)


def _silu(x):
    return x * jax.nn.sigmoid(x)


def _dot(a, b):
    return jnp.dot(a, b, preferred_element_type=F32)


def _dot_nt(a, b):
    return lax.dot_general(a, b, (((1,), (1,)), ((), ())), preferred_element_type=F32)


def _dot_tn(a, b):
    return lax.dot_general(a, b, (((0,), (0,)), ((), ())), preferred_element_type=F32)


def _adaln_kernel(c_ref, w_ref, b_ref, o_ref):
    s = _silu(c_ref[...])
    o_ref[0] = jnp.dot(s, w_ref[0], preferred_element_type=F32,
                       precision=lax.Precision.HIGHEST) + b_ref[0]


def _adaln(cvec, w_ada, b_ada):
    L, D, N6 = w_ada.shape
    tn = 1536
    return pl.pallas_call(
        _adaln_kernel,
        out_shape=jax.ShapeDtypeStruct((L, 8, N6), F32),
        grid=(L, N6 // tn),
        in_specs=[pl.BlockSpec((8, D), lambda l, n: (0, 0)),
                  pl.BlockSpec((1, D, tn), lambda l, n: (l, 0, n)),
                  pl.BlockSpec((1, 1, tn), lambda l, n: (l, 0, n))],
        out_specs=pl.BlockSpec((1, 8, tn), lambda l, n: (l, 0, n)),
        compiler_params=_cparams(("arbitrary", "arbitrary")),
        name="adaln",
    )(cvec, w_ada, b_ada.reshape(L, 1, N6))


def _inproj_kernel(x_ref, mod_ref, cos_ref, sin_ref, qn_ref, kn_ref, ba_ref, w_ref, wvt_ref, wglr_ref,
                   w2_ref, tril_ref, triu_ref,
                   q_o, k_o, vt_o, gk_o, gv_o, gq_o, r_o, y_o, gates_o, bc_o, h_scr, *, rope):
    tm = tril_ref.shape[0]
    for r0 in range(0, h_scr.shape[0], tm):
        _inproj_rows(slice(r0, r0 + tm), tm, rope, x_ref, mod_ref, cos_ref, sin_ref, qn_ref, kn_ref, ba_ref,
                     w_ref, wvt_ref, wglr_ref, w2_ref, tril_ref, triu_ref,
                     q_o, k_o, vt_o, gk_o, gv_o, gq_o, r_o, y_o, gates_o, bc_o, h_scr)


def _inproj_rows(rows, tm, rope, x_ref, mod_ref, cos_ref, sin_ref, qn_ref, kn_ref, ba_ref, w_ref, wvt_ref,
                 wglr_ref, w2_ref, tril_ref, triu_ref,
                 q_o, k_o, vt_o, gk_o, gv_o, gq_o, r_o, y_o, gates_o, bc_o, h_scr):
    shift = mod_ref[0, 0:1, :]
    scale = mod_ref[0, 1:2, :]
    h_scr[rows, :] = (_ln(x_ref[0, rows, :]) * (1.0 + scale) + shift).astype(BF16)

    def mm(c0, width):
        return _dot(h_scr[rows, :], w_ref[:, c0:c0 + width])

    even_lane = lax.broadcasted_iota(jnp.int32, (tm, HEAD_DIM), 1) % 2 == 0

    def norm_rope(xh, gain):
        y = xh * lax.rsqrt(jnp.mean(xh * xh, axis=-1, keepdims=True) + NORM_EPS) * gain
        if rope:
            partner = jnp.where(even_lane, pltpu.roll(y, HEAD_DIM - 1, 1), pltpu.roll(y, 1, 1))
            y = y * cos_ref[rows, :] + partner * sin_ref[rows, :]
        return y

    for c in range(ATT_Q // 512):
        acc = mm(C_Q + c * 512, 512)
        for g in range(4):
            y = norm_rope(acc[:, g * 128:(g + 1) * 128], qn_ref[...]) * (LOG2E * HEAD_DIM ** -0.5)
            q_o[0, rows, c * 512 + g * 128:c * 512 + (g + 1) * 128] = y.astype(BF16)
    acc = mm(C_K, ATT_KV)
    for g in range(N_KV_HEADS):
        y = norm_rope(acc[:, g * 128:(g + 1) * 128], kn_ref[...])
        k_o[0, rows, g * 128:(g + 1) * 128] = y.astype(BF16)
    vt = _dot_nt(wvt_ref[...], h_scr[rows, :])
    for g in range(N_KV_HEADS):
        vt_o[0, g * V_ROWS:g * V_ROWS + HEAD_DIM, rows] = vt[g * 128:(g + 1) * 128].astype(BF16)
        vt_o[0, g * V_ROWS + HEAD_DIM:(g + 1) * V_ROWS, rows] = jnp.ones((V_ROWS - HEAD_DIM, tm), BF16)

    gk_o[0, rows, :] = mm(C_GK, GLA_K).astype(BF16)
    for c in range(GLA_V // 512):
        gv_o[0, rows, c * 512:(c + 1) * 512] = mm(C_GV + c * 512, 512).astype(BF16)
    gq_o[0, rows, :] = (mm(C_GQ, GLA_K) * (GLA_DK ** -0.5)).astype(BF16)
    for c in range(GLA_V // 512):
        r_o[0, rows, c * 512:(c + 1) * 512] = _silu(mm(C_R + c * 512, 512)).astype(BF16)
    for c in range(D_MODEL // 512):
        a = mm(C_GLU_A + c * 512, 512)
        g = mm(C_GLU_G + c * 512, 512)
        y_o[0, rows, c * 512:(c + 1) * 512] = (a * jax.nn.sigmoid(g)).astype(BF16)
    for c in range(3 * D_MODEL // 512):
        gates_o[0, rows, c * 512:(c + 1) * 512] = jax.nn.sigmoid(mm(C_GATES + c * 512, 512)).astype(BF16)

    glr = _dot(h_scr[rows, :], wglr_ref[...])
    z = _dot(glr.astype(BF16), w2_ref[...]) + ba_ref[...]
    lg = (jnp.minimum(z, 0.0) - jnp.log(1.0 + jnp.exp(-jnp.abs(z)))) * (1.0 / GLA_TAU)
    for dr, tri_ref in ((0, tril_ref), (1, triu_ref)):
        g = lg[:, dr * GLA_K:(dr + 1) * GLA_K]
        g1 = g.astype(BF16)
        r1 = g - g1.astype(F32)
        g2 = r1.astype(BF16)
        g3 = (r1 - g2.astype(F32)).astype(BF16)
        tri = tri_ref[...]
        bc_o[dr, 0, rows, :] = _dot(tri, g1) + _dot(tri, g2) + _dot(tri, g3)


def _inproj(x, mod, tabs, p, l, *, rope, tm):
    B, N, D = x.shape
    res = lambda shape: _layer_resident(shape, l)
    nt = N // tm
    row = lambda w: pl.BlockSpec((1, tm, w), lambda b, i: (b, i, 0))
    widths = (ATT_Q, ATT_KV, GLA_K, GLA_V, GLA_K, GLA_V, D_MODEL, 3 * D_MODEL)
    out_shapes = [jax.ShapeDtypeStruct((B, N, w), BF16) for w in widths]
    out_specs = [row(w) for w in widths]
    out_shapes.insert(2, jax.ShapeDtypeStruct((B, N_KV_HEADS * V_ROWS, N), BF16))
    out_specs.insert(2, pl.BlockSpec((1, N_KV_HEADS * V_ROWS, tm), lambda b, i: (b, 0, i)))
    out_shapes.append(jax.ShapeDtypeStruct((2, B, N, GLA_K), F32))
    out_specs.append(pl.BlockSpec((2, 1, tm, GLA_K), lambda b, i: (0, b, i, 0)))
    sub = min(256, tm)
    tril, triu = tabs['tri'][sub]
    return pl.pallas_call(
        functools.partial(_inproj_kernel, rope=rope),
        out_shape=out_shapes,
        grid=(B, nt),
        in_specs=[row(D),
                  pl.BlockSpec((1, 6, D), lambda b, i: (b, 0, 0)),
                  pl.BlockSpec((tm, HEAD_DIM), lambda b, i: (i, 0)),
                  pl.BlockSpec((tm, HEAD_DIM), lambda b, i: (i, 0)),
                  res((1, HEAD_DIM)), res((1, HEAD_DIM)), res((1, 2 * GLA_K)),
                  res((D, C_END)), res((ATT_KV, D)), res((D, 2 * GLA_RANK)),
                  res((2 * GLA_RANK, 2 * GLA_K)),
                  _resident((sub, sub)), _resident((sub, sub))],
        out_specs=out_specs,
        scratch_shapes=[pltpu.VMEM((tm, D), BF16)],
        compiler_params=_cparams(("arbitrary", "arbitrary")),
        name="in_proj_rope" if rope else "in_proj",
    )(x, mod, tabs['cos'], tabs['sin'], p['q_norm'], p['k_norm'], p['b_a'], p['w_in'], p['w_vt'],
      p['w_glr'], p['w_a2'], tril, triu)


def _dwconv(y_ref, yp_ref, yn_ref, w_ref, o_ref, ybuf, ysh, first, last):
    n = y_ref.shape[1]
    span = ysh.shape[1]

    def fill():
        ybuf[0:HALO, :] = jnp.where(first, 0.0, yp_ref[0].astype(F32))
        ybuf[HALO:HALO + n, :] = y_ref[0].astype(F32)
        ybuf[HALO + n:, :] = jnp.where(last, 0.0, yn_ref[0].astype(F32))

    def lane_block(cb, start_zero):
        lanes = slice(cb * 128, (cb + 1) * 128)
        for s in range(1, SUBLANES):
            ysh[s - 1, :, lanes] = ybuf[s:s + span, lanes]
        acc = jnp.broadcast_to(start_zero, (n, 128))
        for t in range(CONV_W):
            a, s = divmod(HALO - CONV_PAD + t, SUBLANES)
            rows = slice(a * SUBLANES, a * SUBLANES + n)
            src = ybuf[rows, lanes] if s == 0 else ysh[s - 1, rows, lanes]
            acc = acc + src * w_ref[t:t + 1, lanes]
        o_ref[0, :, lanes] = acc
        return _zero_after(functools.reduce(jnp.maximum, [acc[r:r + SUBLANES] for r in range(0, n, SUBLANES)]))

    return fill, [functools.partial(lane_block, cb) for cb in range(D_MODEL // 128)]


def _zero_after(x):
    return ((pltpu.bitcast(x, jnp.uint32) >> 16) >> 16).astype(F32)


def _attn_kernel(*refs, src_len, tq, tk):
    n_src = len(src_len)
    q_ref = refs[0]
    src_refs = refs[1:1 + 2 * n_src]
    y_ref, yp_ref, yn_ref, wdw_ref, o_ref, conv_ref = refs[1 + 2 * n_src:7 + 2 * n_src]
    k_scr, vt_scr, kmax_scr, acc_scr, m_scr, ybuf, ysh = refs[7 + 2 * n_src:]
    M = Q_PER_KV * tq
    n_chunk = sum(src_len) // tk
    conv_blk = pl.program_id(1) * pl.num_programs(2) + pl.program_id(2)
    n_conv_blk = pl.num_programs(1) * pl.num_programs(2)

    def rolled(fn):
        def body(i, carry):
            rows = pl.ds(pl.multiple_of(i * tk, tk), tk)
            fn(k_scr[rows, :], vt_scr[:, rows])
            return carry
        lax.fori_loop(0, n_chunk, body, 0)

    @pl.when(pl.program_id(2) == 0)
    def _():
        off = 0
        for s_i, n in enumerate(src_len):
            k_scr[off:off + n, :] = src_refs[2 * s_i][0]
            vt_scr[:, off:off + n] = src_refs[2 * s_i + 1][0]
            off += n
        kmax_scr[...] = jnp.zeros(kmax_scr.shape, F32)

        def knorm(k, vt):
            kf = k.astype(F32)
            n2 = jnp.max(jnp.sum(kf * kf, axis=1, keepdims=True), axis=0, keepdims=True)
            kmax_scr[...] = jnp.maximum(kmax_scr[...], n2)
        rolled(knorm)

    q4 = jnp.concatenate([q_ref[0, :, g * 128:(g + 1) * 128] for g in range(Q_PER_KV)], axis=0)
    qf = q4.astype(F32)
    qn2 = _dot_nt(jnp.ones((SUBLANES, HEAD_DIM), BF16), (qf * qf).astype(BF16))[0:1, :]
    shift = jnp.sqrt(qn2 * kmax_scr[0:1, 0:1])

    conv_fill, conv_pieces = _dwconv(y_ref, yp_ref, yn_ref, wdw_ref, conv_ref, ybuf, ysh,
                                     conv_blk == 0, conv_blk == n_conv_blk - 1)
    conv_fill()
    per_chunk = -(-len(conv_pieces) // max(n_chunk - 2, 1))

    acc = jnp.zeros((V_ROWS, M), F32)
    piece_zeros = {}
    for c in range(n_chunk):
        shift_c = shift
        for zero in piece_zeros.pop(c, []):
            shift_c = jnp.concatenate([shift_c[:, :128] + zero[0:1, :], shift_c[:, 128:]], axis=1)
        rows = slice(c * tk, (c + 1) * tk)
        st = _dot_nt(k_scr[rows, :], q4)
        acc = acc + _dot(vt_scr[:, rows], jnp.exp2(st - shift_c).astype(BF16))
        for piece in conv_pieces[c * per_chunk:(c + 1) * per_chunk]:
            zero = piece(_zero_after(acc[0:1, 0:128]))
            if c + 3 < n_chunk:
                piece_zeros.setdefault(c + 3, []).append(zero)
    acc_scr[...] = acc

    def write_out():
        a = acc_scr[...]
        ot = a[:HEAD_DIM, :] / a[HEAD_DIM:HEAD_DIM + 1, :]
        for g in range(Q_PER_KV):
            o_ref[0, :, g * 128:(g + 1) * 128] = jnp.transpose(ot[:, g * tq:(g + 1) * tq]).astype(BF16)
    write_out()

    l_min = jnp.min(acc_scr[HEAD_DIM:HEAD_DIM + 1, :])

    @pl.when(jnp.logical_not(l_min >= L_MIN))
    def _():
        m_scr[...] = jnp.full(m_scr.shape, -jnp.inf, F32)
        acc_scr[...] = jnp.zeros(acc_scr.shape, F32)

        def online(k, vt):
            st = _dot_nt(k, q4)
            m_prev = m_scr[0:1, :]
            m_new = jnp.maximum(m_prev, jnp.max(st, axis=0, keepdims=True))
            alpha = jnp.exp2(m_prev - m_new)
            acc_scr[...] = alpha * acc_scr[...] + _dot(vt, jnp.exp2(st - m_new).astype(BF16))
            m_scr[...] = jnp.broadcast_to(m_new, m_scr.shape)
        rolled(online)
        write_out()


def _attention(q, srcs, y, w_dw, l, *, tq, tk):
    B, Nq, D = y.shape
    src_len = tuple(k.shape[1] for k, _ in srcs)
    lk = sum(src_len)
    assert lk % tk == 0
    M = Q_PER_KV * tq
    nq = Nq // tq
    cr = tq // N_KV_HEADS
    hb = cr // HALO
    n_halo = Nq // HALO
    conv_blk = lambda h, i: h * nq + i
    in_specs = [pl.BlockSpec((1, tq, Q_PER_KV * HEAD_DIM), lambda b, h, i: (b, i, h))]
    args = [q]
    for (k, vt), n in zip(srcs, src_len):
        in_specs += [pl.BlockSpec((1, n, HEAD_DIM), lambda b, h, i: (b, 0, h)),
                     pl.BlockSpec((1, V_ROWS, n), lambda b, h, i: (b, h, 0))]
        args += [k, vt]
    in_specs += [pl.BlockSpec((1, cr, D), lambda b, h, i: (b, conv_blk(h, i), 0)),
                 pl.BlockSpec((1, HALO, D), lambda b, h, i: (b, jnp.maximum(conv_blk(h, i) * hb - 1, 0), 0)),
                 pl.BlockSpec((1, HALO, D),
                              lambda b, h, i: (b, jnp.minimum((conv_blk(h, i) + 1) * hb, n_halo - 1), 0)),
                 _layer_resident((32, D), l)]
    args += [y, y, y, w_dw]
    return pl.pallas_call(
        functools.partial(_attn_kernel, src_len=src_len, tq=tq, tk=tk),
        out_shape=[jax.ShapeDtypeStruct((B, Nq, ATT_Q), BF16), jax.ShapeDtypeStruct((B, Nq, D), F32)],
        grid=(B, N_KV_HEADS, nq),
        in_specs=in_specs,
        out_specs=[pl.BlockSpec((1, tq, Q_PER_KV * HEAD_DIM), lambda b, h, i: (b, i, h)),
                   pl.BlockSpec((1, cr, D), lambda b, h, i: (b, conv_blk(h, i), 0))],
        scratch_shapes=[pltpu.VMEM((lk, HEAD_DIM), BF16), pltpu.VMEM((V_ROWS, lk), BF16),
                        pltpu.VMEM((SUBLANES, 128), F32), pltpu.VMEM((V_ROWS, M), F32),
                        pltpu.VMEM((SUBLANES, M), F32),
                        pltpu.VMEM((cr + 2 * HALO, D), F32),
                        pltpu.VMEM((SUBLANES - 1, cr + 2 * HALO - SUBLANES, D), F32)],
        compiler_params=_cparams(("arbitrary", "arbitrary", "arbitrary")),
        name="attn%d" % len(srcs),
    )(*args)


def _gla_kernel(qf_ref, kf_ref, vf_ref, bf_ref, qb_ref, kb_ref, vb_ref, bb_ref, tri_ref, s0_ref,
                of_ref, ob_ref, send_ref, s_scr, *, n_chunk):
    j = pl.program_id(1)

    @pl.when(j == 0)
    def _():
        s_scr[...] = s0_ref[:, 0]

    dirs = ((qf_ref, kf_ref, vf_ref, bf_ref, of_ref), (qb_ref, kb_ref, vb_ref, bb_ref, ob_ref))

    def intra_exact(q, k, b):
        row_id = lax.broadcasted_iota(jnp.int32, (GLA_C, GLA_DK), 0)
        col_id = lax.broadcasted_iota(jnp.int32, (GLA_C, GLA_C), 1)

        def one(jrow, a):
            pick = row_id == jrow
            k_j = jnp.sum(jnp.where(pick, k, 0.0), axis=0, keepdims=True)
            b_j = jnp.sum(jnp.where(pick, b, 0.0), axis=0, keepdims=True)
            col = jnp.sum(q * k_j * jnp.exp(jnp.minimum(b - b_j, 0.0)), axis=1, keepdims=True)
            return jnp.where(col_id == jrow, col, a)
        return lax.fori_loop(0, GLA_C, one, jnp.zeros((GLA_C, GLA_C), F32))

    def make_body(exact):
        def body(i, carry):
            for d, (q_ref, k_ref, v_ref, b_ref, o_ref) in enumerate(dirs):
                c = i if d == 0 else n_chunk - 1 - i
                rows = pl.ds(pl.multiple_of(c * GLA_C, GLA_C), GLA_C)
                tri = tri_ref[d]
                for h in range(GLA_HEADS):
                    kc = slice(h * GLA_DK, (h + 1) * GLA_DK)
                    vc = slice(h * GLA_DV, (h + 1) * GLA_DV)
                    b = b_ref[0, 0, rows, kc]
                    q = q_ref[0, rows, kc].astype(F32)
                    k = k_ref[0, rows, kc].astype(F32)
                    v = v_ref[0, rows, vc]
                    b_end = jnp.min(b, axis=0, keepdims=True)
                    qe = (q * jnp.exp(b)).astype(BF16)
                    kh = k * jnp.exp(b_end - b)
                    if exact:
                        a = intra_exact(q, k, b)
                    else:
                        half = 0.5 * b_end
                        a = _dot_nt((q * jnp.exp(b - half)).astype(BF16),
                                    (k * jnp.exp(half - b)).astype(BF16))
                    a = a * tri
                    av = _dot(jnp.concatenate([a.astype(BF16), jnp.transpose(kh).astype(BF16)], axis=0), v)
                    s = s_scr[d, h]
                    o_ref[0, rows, vc] = (_dot(qe, s.astype(BF16)) + av[:GLA_C]).astype(BF16)
                    e_col = jnp.transpose(jnp.broadcast_to(jnp.exp(b_end), (GLA_DK, GLA_DK)))
                    s_scr[d, h] = jnp.concatenate([e_col, e_col], axis=1) * s + av[GLA_C:]
            return carry
        return body

    safe = jnp.minimum(jnp.min(bf_ref[...]), jnp.min(bb_ref[...])) >= -DECAY_SAFE

    @pl.when(safe)
    def _():
        lax.fori_loop(0, n_chunk, make_body(False), 0, unroll=2)

    @pl.when(jnp.logical_not(safe))
    def _():
        lax.fori_loop(0, n_chunk, make_body(True), 0)

    @pl.when(j == pl.num_programs(1) - 1)
    def _():
        send_ref[:, 0] = s_scr[...]


def _gla(gq, gk, gv, bc, s0, tri2, *, tb):
    B, N, _ = gq.shape
    nb = N // tb
    fwd = lambda w: pl.BlockSpec((1, tb, w), lambda b, j: (b, j, 0))
    bwd = lambda w: pl.BlockSpec((1, tb, w), lambda b, j: (b, nb - 1 - j, 0))
    state = pl.BlockSpec((2, 1, GLA_HEADS, GLA_DK, GLA_DV), lambda b, j: (0, b, 0, 0, 0))
    return pl.pallas_call(
        functools.partial(_gla_kernel, n_chunk=tb // GLA_C),
        out_shape=[jax.ShapeDtypeStruct((B, N, GLA_V), BF16), jax.ShapeDtypeStruct((B, N, GLA_V), BF16),
                   jax.ShapeDtypeStruct((2, B, GLA_HEADS, GLA_DK, GLA_DV), F32)],
        grid=(B, nb),
        in_specs=[fwd(GLA_K), fwd(GLA_K), fwd(GLA_V),
                  pl.BlockSpec((1, 1, tb, GLA_K), lambda b, j: (0, b, j, 0)),
                  bwd(GLA_K), bwd(GLA_K), bwd(GLA_V),
                  pl.BlockSpec((1, 1, tb, GLA_K), lambda b, j: (1, b, nb - 1 - j, 0)),
                  _resident((2, GLA_C, GLA_C)), state],
        out_specs=[fwd(GLA_V), bwd(GLA_V), state],
        scratch_shapes=[pltpu.VMEM((2, GLA_HEADS, GLA_DK, GLA_DV), F32)],
        compiler_params=_cparams(("arbitrary", "arbitrary")),
        name="gla",
    )(gq, gk, gv, bc, gq, gk, gv, bc, tri2, s0)


def _merge_kernel(x_ref, mod_ref, oatt_ref, ogf_ref, ogb_ref, r_ref, conv_ref, gates_ref,
                  gn_ref, bdw_ref, clg_ref, clb_ref, watt_ref, wgla_ref, wconv_ref, wout_ref,
                  lng_ref, lnb_ref, xo_ref, *, sub):
    for r0 in range(0, x_ref.shape[1], sub):
        rows = slice(r0, r0 + sub)
        yc = _ln(conv_ref[0, rows, :] + bdw_ref[...]) * clg_ref[...] + clb_ref[...]
        y_conv = _dot(_silu(yc).astype(BF16), wconv_ref[...])

        y_att = _dot(oatt_ref[0, rows, :], watt_ref[...])

        parts = []
        for h in range(GLA_HEADS):
            vc = slice(h * GLA_DV, (h + 1) * GLA_DV)
            o = ogf_ref[0, rows, vc].astype(F32) + ogb_ref[0, rows, vc].astype(F32)
            o = o * lax.rsqrt(jnp.mean(o * o, axis=-1, keepdims=True) + NORM_EPS) * gn_ref[...]
            parts.append((o * r_ref[0, rows, vc].astype(F32)).astype(BF16))
        y_gla = _dot(jnp.concatenate(parts, axis=1), wgla_ref[...])

        g_att = gates_ref[0, rows, 0:D_MODEL].astype(F32)
        g_gla = gates_ref[0, rows, D_MODEL:2 * D_MODEL].astype(F32)
        g_conv = gates_ref[0, rows, 2 * D_MODEL:].astype(F32)
        mix = g_att * y_att + g_gla * y_gla + g_conv * y_conv
        f = _dot(mix.astype(BF16), wout_ref[...])
        gate = mod_ref[0, 2:3, :]
        xo_ref[0, rows, :] = _ln(ALPHA * x_ref[0, rows, :] + gate * f) * lng_ref[...] + lnb_ref[...]


def _merge(x, mod, o_att, o_gla_f, o_gla_b, r, conv, gates, p, l, *, tm):
    B, N, D = x.shape
    row = lambda w: pl.BlockSpec((1, tm, w), lambda b, i: (b, i, 0))
    res = lambda shape: _layer_resident(shape, l)
    vec = res((1, D))
    return pl.pallas_call(
        functools.partial(_merge_kernel, sub=min(256, tm)),
        out_shape=jax.ShapeDtypeStruct((B, N, D), F32),
        grid=(B, N // tm),
        in_specs=[row(D),
                  pl.BlockSpec((1, 6, D), lambda b, i: (b, 0, 0)),
                  row(ATT_Q), row(GLA_V), row(GLA_V), row(GLA_V), row(D), row(3 * D),
                  res((1, GLA_DV)), vec, vec, vec,
                  res((ATT_Q, D)), res((GLA_V, D)), res((D, D)), res((D, D)),
                  vec, vec],
        out_specs=row(D),
        compiler_params=_cparams(("arbitrary", "arbitrary")),
        name="merge",
    )(x, mod, o_att, o_gla_f, o_gla_b, r, conv, gates,
      p['gla_norm'], p['conv_b_dw'], p['conv_ln_g'], p['conv_ln_b'],
      p['w_att_o'], p['w_gla_o'], p['w_conv_o'], p['w_out'], p['ln1_g'], p['ln1_b'])


def _ffn_kernel(x_ref, mod_ref, wg_ref, wu_ref, wd_ref, lng_ref, lnb_ref, xo_ref, h_scr, a_scr, *, tf, sub):
    for r0 in range(0, x_ref.shape[1], sub):
        rows = slice(r0, r0 + sub)
        x = x_ref[0, rows, :]
        h_scr[rows, :] = (_ln(x) * (1.0 + mod_ref[0, 4:5, :]) + mod_ref[0, 3:4, :]).astype(BF16)
        for c0 in range(0, D_FF, tf):
            cols = slice(c0, min(c0 + tf, D_FF))
            g = _dot(h_scr[rows, :], wg_ref[:, cols])
            u = _dot(h_scr[rows, :], wu_ref[:, cols])
            a_scr[rows, cols] = (_silu(g) * u).astype(BF16)
        f = _dot(a_scr[rows, :], wd_ref[...])
        xo_ref[0, rows, :] = _ln(ALPHA * x + mod_ref[0, 5:6, :] * f) * lng_ref[...] + lnb_ref[...]


def _ffn(x, mod, p, l, *, tm):
    B, N, D = x.shape
    row = pl.BlockSpec((1, tm, D), lambda b, i: (b, i, 0))
    res = lambda shape: _layer_resident(shape, l)
    return pl.pallas_call(
        functools.partial(_ffn_kernel, tf=1024, sub=min(256, tm)),
        out_shape=jax.ShapeDtypeStruct((B, N, D), F32),
        grid=(B, N // tm),
        in_specs=[row, pl.BlockSpec((1, 6, D), lambda b, i: (b, 0, 0)),
                  res((D, D_FF)), res((D, D_FF)), res((D_FF, D)), res((1, D)), res((1, D))],
        out_specs=row,
        scratch_shapes=[pltpu.VMEM((tm, D), BF16), pltpu.VMEM((tm, D_FF), BF16)],
        compiler_params=_cparams(("arbitrary", "arbitrary")),
        name="ffn",
    )(x, mod, p['w_ff_gate'], p['w_ff_up'], p['w_ff_down'], p['ln2_g'], p['ln2_b'])


def _tri_blocks(tm):
    idx = np.arange(tm)
    same = (idx[:, None] // GLA_C) == (idx[None, :] // GLA_C)
    tril = same & (idx[None, :] <= idx[:, None])
    triu = same & (idx[None, :] >= idx[:, None])
    return jnp.asarray(tril, BF16), jnp.asarray(triu, BF16)


def _rope_tables(S):
    t = np.arange(S)
    row = (t // GRID_W).astype(np.float32)
    col = (t % GRID_W).astype(np.float32)
    half = HEAD_DIM // 2
    inv = (ROPE_THETA ** (-np.arange(0, half, 2, dtype=np.float32) / half)).astype(np.float32)
    ang = np.concatenate([row[:, None] * inv, col[:, None] * inv], axis=-1).astype(np.float32)
    cos, sin = np.cos(ang), np.sin(ang)
    return (jnp.asarray(np.repeat(cos, 2, axis=-1), F32),
            jnp.asarray(np.stack([-sin, sin], axis=-1).reshape(S, HEAD_DIM), F32))


def _stacked_params(w_in, q_norm, k_norm, w_att_o, gla_w_a2, gla_b_a, gla_norm, w_gla_o, conv_w_dw,
                    conv_b_dw, conv_ln_g, conv_ln_b, w_conv_o, w_out, ln1_g, ln1_b, w_ff_gate, w_ff_up,
                    w_ff_down, ln2_g, ln2_b):
    L = w_in.shape[0]
    o = np.cumsum([0, ATT_KV, ATT_KV, GLA_K, GLA_V, 2 * GLA_RANK, ATT_Q, GLA_K, GLA_V, 2 * D_MODEL,
                   3 * D_MODEL])
    k_w, v_w, gk_w, gv_w, glr_w, q_w, gq_w, r_w, glu_w, gates_w = [w_in[:, :, o[i]:o[i + 1]]
                                                                   for i in range(10)]
    w_main = jnp.concatenate([q_w, k_w, gk_w, gv_w, gq_w, r_w, glu_w, gates_w], axis=2).astype(BF16)
    zero = jnp.zeros((L, GLA_RANK, GLA_K), F32)
    w_a2 = jnp.concatenate([jnp.concatenate([gla_w_a2[:, 0], zero], axis=2),
                            jnp.concatenate([zero, gla_w_a2[:, 1]], axis=2)], axis=1).astype(BF16)
    vec = lambda a: a.reshape(L, 1, -1)
    return {
        'w_in': w_main, 'w_vt': jnp.swapaxes(v_w, 1, 2).astype(BF16), 'w_glr': glr_w.astype(BF16),
        'w_a2': w_a2, 'b_a': vec(gla_b_a), 'q_norm': vec(q_norm), 'k_norm': vec(k_norm),
        'w_att_o': w_att_o.astype(BF16), 'gla_norm': vec(gla_norm), 'w_gla_o': w_gla_o.astype(BF16),
        'conv_w_dw': jnp.pad(conv_w_dw[:, :, 0, :], ((0, 0), (0, 32 - CONV_W), (0, 0))),
        'conv_b_dw': vec(conv_b_dw), 'conv_ln_g': vec(conv_ln_g), 'conv_ln_b': vec(conv_ln_b),
        'w_conv_o': w_conv_o.astype(BF16), 'w_out': w_out.astype(BF16),
        'ln1_g': vec(ln1_g), 'ln1_b': vec(ln1_b),
        'w_ff_gate': w_ff_gate.astype(BF16), 'w_ff_up': w_ff_up.astype(BF16),
        'w_ff_down': w_ff_down.astype(BF16), 'ln2_g': vec(ln2_g), 'ln2_b': vec(ln2_b),
    }


def _key_chunk(n_keys):
    return max(t for t in (256, 512, 768, 1024) if n_keys % t == 0)


def kernel(x, c, ctx, c_ctx, w_ada, b_ada, w_in, q_norm, k_norm, w_att_o, gla_w_a2, gla_b_a, gla_norm,
           w_gla_o, conv_w_dw, conv_b_dw, conv_ln_g, conv_ln_b, w_conv_o, w_out, ln1_g, ln1_b,
           w_ff_gate, w_ff_up, w_ff_down, ln2_g, ln2_b):
    B, S, D = x.shape
    LC = ctx.shape[1]
    tm_lat = min(512, S)
    tm_ctx = min(256, LC)
    cos, sin = _rope_tables(S)
    tri_sizes = sorted({min(256, tm_lat), min(256, tm_ctx)})
    tabs_lat = {'cos': cos, 'sin': sin, 'tri': {t: _tri_blocks(t) for t in tri_sizes}}
    tabs_ctx = {'cos': cos[:LC], 'sin': sin[:LC], 'tri': tabs_lat['tri']}
    idx = np.arange(GLA_C)
    tri2 = jnp.asarray(np.stack([idx[None, :] <= idx[:, None], idx[None, :] >= idx[:, None]]), F32)

    cvec = jnp.concatenate([c, c_ctx[None, :], jnp.zeros((8 - B - 1, D), F32)], axis=0)
    ada = _adaln(cvec, w_ada, b_ada)
    zero_state = jnp.zeros((2, B, GLA_HEADS, GLA_DK, GLA_DV), F32)

    p = _stacked_params(w_in, q_norm, k_norm, w_att_o, gla_w_a2, gla_b_a, gla_norm, w_gla_o, conv_w_dw,
                        conv_b_dw, conv_ln_g, conv_ln_b, w_conv_o, w_out, ln1_g, ln1_b, w_ff_gate,
                        w_ff_up, w_ff_down, ln2_g, ln2_b)
    xc = ctx
    for l in range(DEPTH):
        mod = ada[l, :B].reshape(B, 6, D)
        mod_c = jnp.broadcast_to(ada[l, B].reshape(1, 6, D), (B, 6, D))
        last = l == DEPTH - 1

        qc, kc, vc, gkc, gvc, gqc, rc, yc, gatesc, bcc = _inproj(xc, mod_c, tabs_ctx, p, l, rope=False,
                                                                 tm=tm_ctx)
        ogf_c, ogb_c, s_ctx = _gla(gqc, gkc, gvc, bcc, zero_state, tri2, tb=tm_ctx)
        if not last:
            o_att_c, conv_c = _attention(qc, [(kc, vc)], yc, p['conv_w_dw'], l, tq=min(128, LC),
                                         tk=_key_chunk(LC))
            xc = _merge(xc, mod_c, o_att_c, ogf_c, ogb_c, rc, conv_c, gatesc, p, l, tm=tm_ctx)
            xc = _ffn(xc, mod_c, p, l, tm=tm_ctx)

        q, k, v, gk, gv, gq, r, y, gates, bc = _inproj(x, mod, tabs_lat, p, l, rope=True, tm=tm_lat)
        o_att, conv = _attention(q, [(k, v), (kc, vc)], y, p['conv_w_dw'], l, tq=min(256, S),
                                 tk=_key_chunk(S + LC))
        ogf, ogb, _ = _gla(gq, gk, gv, bc, s_ctx, tri2, tb=min(512, S))
        x = _merge(x, mod, o_att, ogf, ogb, r, conv, gates, p, l, tm=min(512, S))
        x = _ffn(x, mod, p, l, tm=min(512, S))
    return x
```

```python
import functools

import numpy as np
import jax
import jax.numpy as jnp
from jax import lax
from jax.experimental import pallas as pl
from jax.experimental.pallas import tpu as pltpu

F32 = jnp.float32
BF16 = jnp.bfloat16

D_MODEL = 1024
DEPTH = 2
GRID_W = 64
N_HEADS = 8
N_KV_HEADS = 2
HEAD_DIM = 128
Q_PER_KV = N_HEADS // N_KV_HEADS
ROPE_THETA = 10000.0
GLA_HEADS = 4
GLA_DK = 128
GLA_DV = 256
GLA_RANK = 16
GLA_TAU = 16.0
GLA_C = 128
CONV_W = 31
CONV_PAD = CONV_W // 2
HALO = 16
SUBLANES = 8
D_FF = 2816
ALPHA = (2.0 * DEPTH) ** 0.25
NORM_EPS = 1e-6
ATT_Q = N_HEADS * HEAD_DIM
ATT_KV = N_KV_HEADS * HEAD_DIM
V_ROWS = HEAD_DIM + 16
GLA_K = GLA_HEADS * GLA_DK
GLA_V = GLA_HEADS * GLA_DV
DECAY_SAFE = 80.0
LOG2E = 1.4426950408889634
L_MIN = 1e-20

C_K = 0
C_GK = C_K + 2 * ATT_KV
C_GV = C_GK + GLA_K
C_MEM_END = C_GV + GLA_V
C_Q = C_MEM_END
C_GQ = C_Q + ATT_Q
C_R = C_GQ + GLA_K
C_GLU_A = C_R + GLA_V
C_GLU_G = C_GLU_A + D_MODEL
C_GATES = C_GLU_G + D_MODEL
C_END = C_GATES + 3 * D_MODEL

VMEM_LIMIT = 56 * 1024 * 1024
MXU_TILE = 256

ROW_TILE = 512
SUB_TILE = 256
ATT_TQ = 256
PROJ_COLS = 2 * MXU_TILE
FFN_COLS = 4 * MXU_TILE
ADALN_COLS = 6 * MXU_TILE
ADALN_ROWS = SUBLANES
CONV_ROWS = 32


def _cparams(sem):
    return pltpu.CompilerParams(dimension_semantics=sem, vmem_limit_bytes=VMEM_LIMIT)


def _resident(shape):
    nd = len(shape)
    return pl.BlockSpec(shape, lambda *_: (0,) * nd, pipeline_mode=pl.Buffered(1))


def _layer_resident(shape, l):
    nd = len(shape)
    return pl.BlockSpec((None,) + tuple(shape), lambda *_: (l,) + (0,) * nd, pipeline_mode=pl.Buffered(1))


def _ln(x):
    mu = jnp.mean(x, axis=-1, keepdims=True)
    xc = x - mu
    var = jnp.mean(xc * xc, axis=-1, keepdims=True)
    return xc * lax.rsqrt(var + NORM_EPS)


def _silu(x):
    return x * jax.nn.sigmoid(x)


def _dot(a, b):
    return jnp.dot(a, b, preferred_element_type=F32)


def _dot_nt(a, b):
    return lax.dot_general(a, b, (((1,), (1,)), ((), ())), preferred_element_type=F32)


def _dot_tn(a, b):
    return lax.dot_general(a, b, (((0,), (0,)), ((), ())), preferred_element_type=F32)


def _adaln_kernel(c_ref, w_ref, b_ref, o_ref):
    s = _silu(c_ref[...])
    o_ref[0] = jnp.dot(s, w_ref[0], preferred_element_type=F32,
                       precision=lax.Precision.HIGHEST) + b_ref[0]


def _adaln(cvec, w_ada, b_ada):
    L, D, N6 = w_ada.shape
    tn = ADALN_COLS
    return pl.pallas_call(
        _adaln_kernel,
        out_shape=jax.ShapeDtypeStruct((L, ADALN_ROWS, N6), F32),
        grid=(L, N6 // tn),
        in_specs=[pl.BlockSpec((ADALN_ROWS, D), lambda l, n: (0, 0)),
                  pl.BlockSpec((1, D, tn), lambda l, n: (l, 0, n)),
                  pl.BlockSpec((1, 1, tn), lambda l, n: (l, 0, n))],
        out_specs=pl.BlockSpec((1, ADALN_ROWS, tn), lambda l, n: (l, 0, n)),
        compiler_params=_cparams(("arbitrary", "arbitrary")),
        name="adaln",
    )(cvec, w_ada, b_ada.reshape(L, 1, N6))


def _inproj_kernel(x_ref, mod_ref, cos_ref, sin_ref, qn_ref, kn_ref, ba_ref, wmem_ref, wrest_ref, wvt_ref,
                   wglr_ref, w2_ref, tril_ref, triu_ref,
                   q_o, k_o, vt_o, gk_o, gv_o, gq_o, r_o, y_o, gates_o, bc_o, h_scr, *, rope):
    tm = tril_ref.shape[0]
    for r0 in range(0, h_scr.shape[0], tm):
        _inproj_rows(slice(r0, r0 + tm), tm, rope, x_ref, mod_ref, cos_ref, sin_ref, qn_ref, kn_ref, ba_ref,
                     (wmem_ref, wrest_ref), wvt_ref, wglr_ref, w2_ref, tril_ref, triu_ref,
                     q_o, k_o, vt_o, gk_o, gv_o, gq_o, r_o, y_o, gates_o, bc_o, h_scr)


def _inproj_rows(rows, tm, rope, x_ref, mod_ref, cos_ref, sin_ref, qn_ref, kn_ref, ba_ref, w_ref, wvt_ref,
                 wglr_ref, w2_ref, tril_ref, triu_ref,
                 q_o, k_o, vt_o, gk_o, gv_o, gq_o, r_o, y_o, gates_o, bc_o, h_scr):
    shift = mod_ref[0, 0:1, :]
    scale = mod_ref[0, 1:2, :]
    h_scr[rows, :] = (_ln(x_ref[0, rows, :]) * (1.0 + scale) + shift).astype(BF16)

    def mm(c0, width):
        ref, base = (w_ref[0], 0) if c0 < C_MEM_END else (w_ref[1], C_MEM_END)
        return _dot(h_scr[rows, :], ref[:, c0 - base:c0 - base + width])

    even_lane = lax.broadcasted_iota(jnp.int32, (tm, HEAD_DIM), 1) % 2 == 0

    def norm_rope(xh, gain):
        y = xh * lax.rsqrt(jnp.mean(xh * xh, axis=-1, keepdims=True) + NORM_EPS) * gain
        if rope:
            partner = jnp.where(even_lane, pltpu.roll(y, HEAD_DIM - 1, 1), pltpu.roll(y, 1, 1))
            y = y * cos_ref[rows, :] + partner * sin_ref[rows, :]
        return y

    for c in range(ATT_Q // PROJ_COLS):
        acc = mm(C_Q + c * PROJ_COLS, PROJ_COLS)
        for g in range(4):
            y = norm_rope(acc[:, g * 128:(g + 1) * 128], qn_ref[...]) * (LOG2E * HEAD_DIM ** -0.5)
            q_o[0, rows, c * PROJ_COLS + g * 128:c * PROJ_COLS + (g + 1) * 128] = y.astype(BF16)
    acc = mm(C_K, ATT_KV)
    for g in range(N_KV_HEADS):
        y = norm_rope(acc[:, g * 128:(g + 1) * 128], kn_ref[...])
        k_o[0, rows, g * 128:(g + 1) * 128] = y.astype(BF16)
    vt = _dot_nt(wvt_ref[...], h_scr[rows, :])
    for g in range(N_KV_HEADS):
        vt_o[0, g * V_ROWS:g * V_ROWS + HEAD_DIM, rows] = vt[g * 128:(g + 1) * 128].astype(BF16)
        vt_o[0, g * V_ROWS + HEAD_DIM:(g + 1) * V_ROWS, rows] = jnp.ones((V_ROWS - HEAD_DIM, tm), BF16)

    gk_o[0, rows, :] = mm(C_GK, GLA_K).astype(BF16)
    for c in range(GLA_V // PROJ_COLS):
        gv_o[0, rows, c * PROJ_COLS:(c + 1) * PROJ_COLS] = mm(C_GV + c * PROJ_COLS, PROJ_COLS).astype(BF16)
    gq_o[0, rows, :] = (mm(C_GQ, GLA_K) * (GLA_DK ** -0.5)).astype(BF16)
    for c in range(GLA_V // PROJ_COLS):
        r_o[0, rows, c * PROJ_COLS:(c + 1) * PROJ_COLS] = _silu(mm(C_R + c * PROJ_COLS, PROJ_COLS)).astype(BF16)
    for c in range(D_MODEL // PROJ_COLS):
        a = mm(C_GLU_A + c * PROJ_COLS, PROJ_COLS)
        g = mm(C_GLU_G + c * PROJ_COLS, PROJ_COLS)
        y_o[0, rows, c * PROJ_COLS:(c + 1) * PROJ_COLS] = (a * jax.nn.sigmoid(g)).astype(BF16)
    for c in range(3 * D_MODEL // PROJ_COLS):
        gates_o[0, rows, c * PROJ_COLS:(c + 1) * PROJ_COLS] = jax.nn.sigmoid(mm(C_GATES + c * PROJ_COLS, PROJ_COLS)).astype(BF16)

    glr = _dot(h_scr[rows, :], wglr_ref[...])
    z = _dot(glr.astype(BF16), w2_ref[...]) + ba_ref[...]
    lg = (jnp.minimum(z, 0.0) - jnp.log(1.0 + jnp.exp(-jnp.abs(z)))) * (1.0 / GLA_TAU)
    for dr, tri_ref in ((0, tril_ref), (1, triu_ref)):
        g = lg[:, dr * GLA_K:(dr + 1) * GLA_K]
        g1 = g.astype(BF16)
        r1 = g - g1.astype(F32)
        g2 = r1.astype(BF16)
        g3 = (r1 - g2.astype(F32)).astype(BF16)
        tri = tri_ref[...]
        bc_o[dr, 0, rows, :] = _dot(tri, g1) + _dot(tri, g2) + _dot(tri, g3)


def _inproj(x, mod, tabs, p, l, *, rope, tm):
    B, N, D = x.shape
    res = lambda shape: _layer_resident(shape, l)
    nt = N // tm
    row = lambda w: pl.BlockSpec((1, tm, w), lambda b, i: (b, i, 0))
    widths = (ATT_Q, ATT_KV, GLA_K, GLA_V, GLA_K, GLA_V, D_MODEL, 3 * D_MODEL)
    out_shapes = [jax.ShapeDtypeStruct((B, N, w), BF16) for w in widths]
    out_specs = [row(w) for w in widths]
    out_shapes.insert(2, jax.ShapeDtypeStruct((B, N_KV_HEADS * V_ROWS, N), BF16))
    out_specs.insert(2, pl.BlockSpec((1, N_KV_HEADS * V_ROWS, tm), lambda b, i: (b, 0, i)))
    out_shapes.append(jax.ShapeDtypeStruct((2, B, N, GLA_K), F32))
    out_specs.append(pl.BlockSpec((2, 1, tm, GLA_K), lambda b, i: (0, b, i, 0)))
    sub = min(SUB_TILE, tm)
    tril, triu = tabs['tri'][sub]
    return pl.pallas_call(
        functools.partial(_inproj_kernel, rope=rope),
        out_shape=out_shapes,
        grid=(B, nt),
        in_specs=[row(D),
                  pl.BlockSpec((1, 6, D), lambda b, i: (b, 0, 0)),
                  pl.BlockSpec((tm, HEAD_DIM), lambda b, i: (i, 0)),
                  pl.BlockSpec((tm, HEAD_DIM), lambda b, i: (i, 0)),
                  res((1, HEAD_DIM)), res((1, HEAD_DIM)), res((1, 2 * GLA_K)),
                  res((D, C_MEM_END)), res((D, C_END - C_MEM_END)), res((ATT_KV, D)), res((D, 2 * GLA_RANK)),
                  res((2 * GLA_RANK, 2 * GLA_K)),
                  _resident((sub, sub)), _resident((sub, sub))],
        out_specs=out_specs,
        scratch_shapes=[pltpu.VMEM((tm, D), BF16)],
        compiler_params=_cparams(("arbitrary", "arbitrary")),
        name="in_proj_rope" if rope else "in_proj",
    )(x, mod, tabs['cos'], tabs['sin'], p['q_norm'], p['k_norm'], p['b_a'], p['w_mem'], p['w_rest'],
      p['w_vt'], p['w_glr'], p['w_a2'], tril, triu)


def _dwconv(y_ref, yp_ref, yn_ref, w_ref, o_ref, ybuf, ysh, first, last):
    n = y_ref.shape[1]
    span = ysh.shape[1]

    def fill():
        ybuf[0:HALO, :] = jnp.where(first, 0.0, yp_ref[0].astype(F32))
        ybuf[HALO:HALO + n, :] = y_ref[0].astype(F32)
        ybuf[HALO + n:, :] = jnp.where(last, 0.0, yn_ref[0].astype(F32))

    def lane_block(cb, start_zero):
        lanes = slice(cb * 128, (cb + 1) * 128)
        for s in range(1, SUBLANES):
            ysh[s - 1, :, lanes] = ybuf[s:s + span, lanes]
        acc = jnp.broadcast_to(start_zero, (n, 128))
        for t in range(CONV_W):
            a, s = divmod(HALO - CONV_PAD + t, SUBLANES)
            rows = slice(a * SUBLANES, a * SUBLANES + n)
            src = ybuf[rows, lanes] if s == 0 else ysh[s - 1, rows, lanes]
            acc = acc + src * w_ref[t:t + 1, lanes]
        o_ref[0, :, lanes] = acc
        return _zero_after(functools.reduce(jnp.maximum, [acc[r:r + SUBLANES] for r in range(0, n, SUBLANES)]))

    return fill, [functools.partial(lane_block, cb) for cb in range(D_MODEL // 128)]


def _zero_after(x):
    return ((pltpu.bitcast(x, jnp.uint32) >> 16) >> 16).astype(F32)


def _attn_kernel(*refs, src_len, tq, tk):
    n_src = len(src_len)
    q_ref = refs[0]
    src_refs = refs[1:1 + 2 * n_src]
    y_ref, yp_ref, yn_ref, wdw_ref, o_ref, conv_ref = refs[1 + 2 * n_src:7 + 2 * n_src]
    k_scr, vt_scr, kmax_scr, acc_scr, m_scr, ybuf, ysh = refs[7 + 2 * n_src:]
    M = Q_PER_KV * tq
    n_chunk = sum(src_len) // tk
    conv_blk = pl.program_id(1) * pl.num_programs(2) + pl.program_id(2)
    n_conv_blk = pl.num_programs(1) * pl.num_programs(2)

    def rolled(fn):
        def body(i, carry):
            rows = pl.ds(pl.multiple_of(i * tk, tk), tk)
            fn(k_scr[rows, :], vt_scr[:, rows])
            return carry
        lax.fori_loop(0, n_chunk, body, 0)

    @pl.when(pl.program_id(2) == 0)
    def _():
        off = 0
        for s_i, n in enumerate(src_len):
            k_scr[off:off + n, :] = src_refs[2 * s_i][0]
            vt_scr[:, off:off + n] = src_refs[2 * s_i + 1][0]
            off += n
        kmax_scr[...] = jnp.zeros(kmax_scr.shape, F32)

        def knorm(k, vt):
            kf = k.astype(F32)
            n2 = jnp.max(jnp.sum(kf * kf, axis=1, keepdims=True), axis=0, keepdims=True)
            kmax_scr[...] = jnp.maximum(kmax_scr[...], n2)
        rolled(knorm)

    q4 = jnp.concatenate([q_ref[0, :, g * 128:(g + 1) * 128] for g in range(Q_PER_KV)], axis=0)
    qf = q4.astype(F32)
    qn2 = _dot_nt(jnp.ones((SUBLANES, HEAD_DIM), BF16), (qf * qf).astype(BF16))[0:1, :]
    shift = jnp.sqrt(qn2 * kmax_scr[0:1, 0:1])

    conv_fill, conv_pieces = _dwconv(y_ref, yp_ref, yn_ref, wdw_ref, conv_ref, ybuf, ysh,
                                     conv_blk == 0, conv_blk == n_conv_blk - 1)
    conv_fill()
    per_chunk = -(-len(conv_pieces) // max(n_chunk - 2, 1))

    acc = jnp.zeros((V_ROWS, M), F32)
    piece_zeros = {}
    for c in range(n_chunk):
        shift_c = shift
        for zero in piece_zeros.pop(c, []):
            shift_c = jnp.concatenate([shift_c[:, :128] + zero[0:1, :], shift_c[:, 128:]], axis=1)
        rows = slice(c * tk, (c + 1) * tk)
        st = _dot_nt(k_scr[rows, :], q4)
        acc = acc + _dot(vt_scr[:, rows], jnp.exp2(st - shift_c).astype(BF16))
        for piece in conv_pieces[c * per_chunk:(c + 1) * per_chunk]:
            zero = piece(_zero_after(acc[0:1, 0:128]))
            if c + 3 < n_chunk:
                piece_zeros.setdefault(c + 3, []).append(zero)
    acc_scr[...] = acc

    def write_out():
        a = acc_scr[...]
        ot = a[:HEAD_DIM, :] / a[HEAD_DIM:HEAD_DIM + 1, :]
        for g in range(Q_PER_KV):
            o_ref[0, :, g * 128:(g + 1) * 128] = jnp.transpose(ot[:, g * tq:(g + 1) * tq]).astype(BF16)
    write_out()

    l_min = jnp.min(acc_scr[HEAD_DIM:HEAD_DIM + 1, :])

    @pl.when(jnp.logical_not(l_min >= L_MIN))
    def _():
        m_scr[...] = jnp.full(m_scr.shape, -jnp.inf, F32)
        acc_scr[...] = jnp.zeros(acc_scr.shape, F32)

        def online(k, vt):
            st = _dot_nt(k, q4)
            m_prev = m_scr[0:1, :]
            m_new = jnp.maximum(m_prev, jnp.max(st, axis=0, keepdims=True))
            alpha = jnp.exp2(m_prev - m_new)
            acc_scr[...] = alpha * acc_scr[...] + _dot(vt, jnp.exp2(st - m_new).astype(BF16))
            m_scr[...] = jnp.broadcast_to(m_new, m_scr.shape)
        rolled(online)
        write_out()


def _attention(q, srcs, y, w_dw, l, *, tq, tk):
    B, Nq, D = y.shape
    src_len = tuple(k.shape[1] for k, _ in srcs)
    lk = sum(src_len)
    assert lk % tk == 0
    M = Q_PER_KV * tq
    nq = Nq // tq
    cr = tq // N_KV_HEADS
    hb = cr // HALO
    n_halo = Nq // HALO
    conv_blk = lambda h, i: h * nq + i
    in_specs = [pl.BlockSpec((1, tq, Q_PER_KV * HEAD_DIM), lambda b, h, i: (b, i, h))]
    args = [q]
    for (k, vt), n in zip(srcs, src_len):
        in_specs += [pl.BlockSpec((1, n, HEAD_DIM), lambda b, h, i: (b, 0, h)),
                     pl.BlockSpec((1, V_ROWS, n), lambda b, h, i: (b, h, 0))]
        args += [k, vt]
    in_specs += [pl.BlockSpec((1, cr, D), lambda b, h, i: (b, conv_blk(h, i), 0)),
                 pl.BlockSpec((1, HALO, D), lambda b, h, i: (b, jnp.maximum(conv_blk(h, i) * hb - 1, 0), 0)),
                 pl.BlockSpec((1, HALO, D),
                              lambda b, h, i: (b, jnp.minimum((conv_blk(h, i) + 1) * hb, n_halo - 1), 0)),
                 _layer_resident((CONV_ROWS, D), l)]
    args += [y, y, y, w_dw]
    return pl.pallas_call(
        functools.partial(_attn_kernel, src_len=src_len, tq=tq, tk=tk),
        out_shape=[jax.ShapeDtypeStruct((B, Nq, ATT_Q), BF16), jax.ShapeDtypeStruct((B, Nq, D), F32)],
        grid=(B, N_KV_HEADS, nq),
        in_specs=in_specs,
        out_specs=[pl.BlockSpec((1, tq, Q_PER_KV * HEAD_DIM), lambda b, h, i: (b, i, h)),
                   pl.BlockSpec((1, cr, D), lambda b, h, i: (b, conv_blk(h, i), 0))],
        scratch_shapes=[pltpu.VMEM((lk, HEAD_DIM), BF16), pltpu.VMEM((V_ROWS, lk), BF16),
                        pltpu.VMEM((SUBLANES, 128), F32), pltpu.VMEM((V_ROWS, M), F32),
                        pltpu.VMEM((SUBLANES, M), F32),
                        pltpu.VMEM((cr + 2 * HALO, D), F32),
                        pltpu.VMEM((SUBLANES - 1, cr + 2 * HALO - SUBLANES, D), F32)],
        compiler_params=_cparams(("arbitrary", "arbitrary", "arbitrary")),
        name="attn%d" % len(srcs),
    )(*args)


def _gla_kernel(qf_ref, kf_ref, vf_ref, bf_ref, qb_ref, kb_ref, vb_ref, bb_ref, tri_ref, s0_ref,
                of_ref, ob_ref, send_ref, s_scr, *, n_chunk):
    j = pl.program_id(1)

    @pl.when(j == 0)
    def _():
        s_scr[...] = s0_ref[:, 0]

    dirs = ((qf_ref, kf_ref, vf_ref, bf_ref, of_ref), (qb_ref, kb_ref, vb_ref, bb_ref, ob_ref))

    def intra_exact(q, k, b):
        row_id = lax.broadcasted_iota(jnp.int32, (GLA_C, GLA_DK), 0)
        col_id = lax.broadcasted_iota(jnp.int32, (GLA_C, GLA_C), 1)

        def one(jrow, a):
            pick = row_id == jrow
            k_j = jnp.sum(jnp.where(pick, k, 0.0), axis=0, keepdims=True)
            b_j = jnp.sum(jnp.where(pick, b, 0.0), axis=0, keepdims=True)
            col = jnp.sum(q * k_j * jnp.exp(jnp.minimum(b - b_j, 0.0)), axis=1, keepdims=True)
            return jnp.where(col_id == jrow, col, a)
        return lax.fori_loop(0, GLA_C, one, jnp.zeros((GLA_C, GLA_C), F32))

    def make_body(exact):
        def body(i, carry):
            for d, (q_ref, k_ref, v_ref, b_ref, o_ref) in enumerate(dirs):
                c = i if d == 0 else n_chunk - 1 - i
                rows = pl.ds(pl.multiple_of(c * GLA_C, GLA_C), GLA_C)
                tri = tri_ref[d]
                for h in range(GLA_HEADS):
                    kc = slice(h * GLA_DK, (h + 1) * GLA_DK)
                    vc = slice(h * GLA_DV, (h + 1) * GLA_DV)
                    b = b_ref[0, 0, rows, kc]
                    q = q_ref[0, rows, kc].astype(F32)
                    k = k_ref[0, rows, kc].astype(F32)
                    v = v_ref[0, rows, vc]
                    b_end = jnp.min(b, axis=0, keepdims=True)
                    qe = q * jnp.exp(b)
                    kh = k * jnp.exp(b_end - b)
                    if exact:
                        a = intra_exact(q, k, b)
                    else:
                        mid = jnp.exp(-0.5 * b_end)
                        a = _dot_nt((qe * mid).astype(BF16), (kh * mid).astype(BF16))
                    qe = qe.astype(BF16)
                    a = a * tri
                    av = _dot(jnp.concatenate([a.astype(BF16), jnp.transpose(kh).astype(BF16)], axis=0), v)
                    s = s_scr[d, h]
                    o_ref[0, rows, vc] = (_dot(qe, s.astype(BF16)) + av[:GLA_C]).astype(BF16)
                    e_col = jnp.transpose(jnp.broadcast_to(jnp.exp(b_end), (GLA_DK, GLA_DK)))
                    s_scr[d, h] = jnp.concatenate([e_col, e_col], axis=1) * s + av[GLA_C:]
            return carry
        return body

    safe = jnp.minimum(jnp.min(bf_ref[...]), jnp.min(bb_ref[...])) >= -DECAY_SAFE

    @pl.when(safe)
    def _():
        lax.fori_loop(0, n_chunk, make_body(False), 0, unroll=2)

    @pl.when(jnp.logical_not(safe))
    def _():
        lax.fori_loop(0, n_chunk, make_body(True), 0)

    @pl.when(j == pl.num_programs(1) - 1)
    def _():
        send_ref[:, 0] = s_scr[...]


def _gla(gq, gk, gv, bc, s0, tri2, *, tb):
    B, N, _ = gq.shape
    nb = N // tb
    fwd = lambda w: pl.BlockSpec((1, tb, w), lambda b, j: (b, j, 0))
    bwd = lambda w: pl.BlockSpec((1, tb, w), lambda b, j: (b, nb - 1 - j, 0))
    state = pl.BlockSpec((2, 1, GLA_HEADS, GLA_DK, GLA_DV), lambda b, j: (0, b, 0, 0, 0))
    return pl.pallas_call(
        functools.partial(_gla_kernel, n_chunk=tb // GLA_C),
        out_shape=[jax.ShapeDtypeStruct((B, N, GLA_V), BF16), jax.ShapeDtypeStruct((B, N, GLA_V), BF16),
                   jax.ShapeDtypeStruct((2, B, GLA_HEADS, GLA_DK, GLA_DV), F32)],
        grid=(B, nb),
        in_specs=[fwd(GLA_K), fwd(GLA_K), fwd(GLA_V),
                  pl.BlockSpec((1, 1, tb, GLA_K), lambda b, j: (0, b, j, 0)),
                  bwd(GLA_K), bwd(GLA_K), bwd(GLA_V),
                  pl.BlockSpec((1, 1, tb, GLA_K), lambda b, j: (1, b, nb - 1 - j, 0)),
                  _resident((2, GLA_C, GLA_C)), state],
        out_specs=[fwd(GLA_V), bwd(GLA_V), state],
        scratch_shapes=[pltpu.VMEM((2, GLA_HEADS, GLA_DK, GLA_DV), F32)],
        compiler_params=_cparams(("arbitrary", "arbitrary")),
        name="gla",
    )(gq, gk, gv, bc, gq, gk, gv, bc, tri2, s0)


def _merge_kernel(x_ref, mod_ref, oatt_ref, ogf_ref, ogb_ref, r_ref, conv_ref, gates_ref,
                  gn_ref, bdw_ref, clg_ref, clb_ref, watt_ref, wgla_ref, wconv_ref, wout_ref,
                  lng_ref, lnb_ref, xo_ref, *, sub):
    for r0 in range(0, x_ref.shape[1], sub):
        rows = slice(r0, r0 + sub)
        yc = _ln(conv_ref[0, rows, :] + bdw_ref[...]) * clg_ref[...] + clb_ref[...]
        y_conv = _dot(_silu(yc).astype(BF16), wconv_ref[...])

        y_att = _dot(oatt_ref[0, rows, :], watt_ref[...])

        parts = []
        for h in range(GLA_HEADS):
            vc = slice(h * GLA_DV, (h + 1) * GLA_DV)
            o = ogf_ref[0, rows, vc].astype(F32) + ogb_ref[0, rows, vc].astype(F32)
            o = o * lax.rsqrt(jnp.mean(o * o, axis=-1, keepdims=True) + NORM_EPS) * gn_ref[...]
            parts.append((o * r_ref[0, rows, vc].astype(F32)).astype(BF16))
        y_gla = _dot(jnp.concatenate(parts, axis=1), wgla_ref[...])

        g_att = gates_ref[0, rows, 0:D_MODEL].astype(F32)
        g_gla = gates_ref[0, rows, D_MODEL:2 * D_MODEL].astype(F32)
        g_conv = gates_ref[0, rows, 2 * D_MODEL:].astype(F32)
        mix = g_att * y_att + g_gla * y_gla + g_conv * y_conv
        f = _dot(mix.astype(BF16), wout_ref[...])
        gate = mod_ref[0, 2:3, :]
        xo_ref[0, rows, :] = _ln(ALPHA * x_ref[0, rows, :] + gate * f) * lng_ref[...] + lnb_ref[...]


def _merge(x, mod, o_att, o_gla_f, o_gla_b, r, conv, gates, p, l, *, tm):
    B, N, D = x.shape
    row = lambda w: pl.BlockSpec((1, tm, w), lambda b, i: (b, i, 0))
    res = lambda shape: _layer_resident(shape, l)
    vec = res((1, D))
    return pl.pallas_call(
        functools.partial(_merge_kernel, sub=min(SUB_TILE, tm)),
        out_shape=jax.ShapeDtypeStruct((B, N, D), F32),
        grid=(B, N // tm),
        in_specs=[row(D),
                  pl.BlockSpec((1, 6, D), lambda b, i: (b, 0, 0)),
                  row(ATT_Q), row(GLA_V), row(GLA_V), row(GLA_V), row(D), row(3 * D),
                  res((1, GLA_DV)), vec, vec, vec,
                  res((ATT_Q, D)), res((GLA_V, D)), res((D, D)), res((D, D)),
                  vec, vec],
        out_specs=row(D),
        compiler_params=_cparams(("arbitrary", "arbitrary")),
        name="merge",
    )(x, mod, o_att, o_gla_f, o_gla_b, r, conv, gates,
      p['gla_norm'], p['conv_b_dw'], p['conv_ln_g'], p['conv_ln_b'],
      p['w_att_o'], p['w_gla_o'], p['w_conv_o'], p['w_out'], p['ln1_g'], p['ln1_b'])


def _ffn_kernel(x_ref, mod_ref, wg_ref, wu_ref, wd_ref, lng_ref, lnb_ref, xo_ref, h_scr, a_scr, *, tf, sub):
    for r0 in range(0, x_ref.shape[1], sub):
        rows = slice(r0, r0 + sub)
        x = x_ref[0, rows, :]
        h_scr[rows, :] = (_ln(x) * (1.0 + mod_ref[0, 4:5, :]) + mod_ref[0, 3:4, :]).astype(BF16)
        for c0 in range(0, D_FF, tf):
            cols = slice(c0, min(c0 + tf, D_FF))
            g = _dot(h_scr[rows, :], wg_ref[:, cols])
            u = _dot(h_scr[rows, :], wu_ref[:, cols])
            a_scr[rows, cols] = (_silu(g) * u).astype(BF16)
        f = _dot(a_scr[rows, :], wd_ref[...])
        xo_ref[0, rows, :] = _ln(ALPHA * x + mod_ref[0, 5:6, :] * f) * lng_ref[...] + lnb_ref[...]


def _ffn(x, mod, p, l, *, tm):
    B, N, D = x.shape
    row = pl.BlockSpec((1, tm, D), lambda b, i: (b, i, 0))
    res = lambda shape: _layer_resident(shape, l)
    return pl.pallas_call(
        functools.partial(_ffn_kernel, tf=FFN_COLS, sub=min(SUB_TILE, tm)),
        out_shape=jax.ShapeDtypeStruct((B, N, D), F32),
        grid=(B, N // tm),
        in_specs=[row, pl.BlockSpec((1, 6, D), lambda b, i: (b, 0, 0)),
                  res((D, D_FF)), res((D, D_FF)), res((D_FF, D)), res((1, D)), res((1, D))],
        out_specs=row,
        scratch_shapes=[pltpu.VMEM((tm, D), BF16), pltpu.VMEM((tm, D_FF), BF16)],
        compiler_params=_cparams(("arbitrary", "arbitrary")),
        name="ffn",
    )(x, mod, p['w_ff_gate'], p['w_ff_up'], p['w_ff_down'], p['ln2_g'], p['ln2_b'])


def _tri_blocks(tm):
    idx = np.arange(tm)
    same = (idx[:, None] // GLA_C) == (idx[None, :] // GLA_C)
    tril = same & (idx[None, :] <= idx[:, None])
    triu = same & (idx[None, :] >= idx[:, None])
    return jnp.asarray(tril, BF16), jnp.asarray(triu, BF16)


def _rope_tables(S):
    t = np.arange(S)
    row = (t // GRID_W).astype(np.float32)
    col = (t % GRID_W).astype(np.float32)
    half = HEAD_DIM // 2
    inv = (ROPE_THETA ** (-np.arange(0, half, 2, dtype=np.float32) / half)).astype(np.float32)
    ang = np.concatenate([row[:, None] * inv, col[:, None] * inv], axis=-1).astype(np.float32)
    cos, sin = np.cos(ang), np.sin(ang)
    return (jnp.asarray(np.repeat(cos, 2, axis=-1), F32),
            jnp.asarray(np.stack([-sin, sin], axis=-1).reshape(S, HEAD_DIM), F32))


def _stacked_params(w_in, q_norm, k_norm, w_att_o, gla_w_a2, gla_b_a, gla_norm, w_gla_o, conv_w_dw,
                    conv_b_dw, conv_ln_g, conv_ln_b, w_conv_o, w_out, ln1_g, ln1_b, w_ff_gate, w_ff_up,
                    w_ff_down, ln2_g, ln2_b):
    L = w_in.shape[0]
    glr0 = C_MEM_END
    glr1 = glr0 + 2 * GLA_RANK
    v_w = w_in[:, :, ATT_KV:2 * ATT_KV]
    zero = jnp.zeros((L, GLA_RANK, GLA_K), F32)
    w_a2 = jnp.concatenate([jnp.concatenate([gla_w_a2[:, 0], zero], axis=2),
                            jnp.concatenate([zero, gla_w_a2[:, 1]], axis=2)], axis=1).astype(BF16)
    vec = lambda a: a.reshape(L, 1, -1)
    return {
        'w_mem': w_in[:, :, :glr0].astype(BF16), 'w_rest': w_in[:, :, glr1:].astype(BF16),
        'w_vt': jnp.swapaxes(v_w, 1, 2).astype(BF16), 'w_glr': w_in[:, :, glr0:glr1].astype(BF16),
        'w_a2': w_a2, 'b_a': vec(gla_b_a), 'q_norm': vec(q_norm), 'k_norm': vec(k_norm),
        'w_att_o': w_att_o.astype(BF16), 'gla_norm': vec(gla_norm), 'w_gla_o': w_gla_o.astype(BF16),
        'conv_w_dw': jnp.pad(conv_w_dw[:, :, 0, :], ((0, 0), (0, CONV_ROWS - CONV_W), (0, 0))),
        'conv_b_dw': vec(conv_b_dw), 'conv_ln_g': vec(conv_ln_g), 'conv_ln_b': vec(conv_ln_b),
        'w_conv_o': w_conv_o.astype(BF16), 'w_out': w_out.astype(BF16),
        'ln1_g': vec(ln1_g), 'ln1_b': vec(ln1_b),
        'w_ff_gate': w_ff_gate.astype(BF16), 'w_ff_up': w_ff_up.astype(BF16),
        'w_ff_down': w_ff_down.astype(BF16), 'ln2_g': vec(ln2_g), 'ln2_b': vec(ln2_b),
    }


def _key_chunk(n_keys):
    return max(t for t in range(MXU_TILE, 5 * MXU_TILE, MXU_TILE) if n_keys % t == 0)


def kernel(x, c, ctx, c_ctx, w_ada, b_ada, w_in, q_norm, k_norm, w_att_o, gla_w_a2, gla_b_a, gla_norm,
           w_gla_o, conv_w_dw, conv_b_dw, conv_ln_g, conv_ln_b, w_conv_o, w_out, ln1_g, ln1_b,
           w_ff_gate, w_ff_up, w_ff_down, ln2_g, ln2_b):
    B, S, D = x.shape
    LC = ctx.shape[1]
    tm_lat = min(ROW_TILE, S)
    tm_ctx = min(SUB_TILE, LC)
    tq_ctx = min(ATT_TQ // 2, LC)
    cos, sin = _rope_tables(S)
    tri_sizes = sorted({min(SUB_TILE, tm_lat), min(SUB_TILE, tm_ctx)})
    tabs_lat = {'cos': cos, 'sin': sin, 'tri': {t: _tri_blocks(t) for t in tri_sizes}}
    tabs_ctx = {'cos': cos[:LC], 'sin': sin[:LC], 'tri': tabs_lat['tri']}
    idx = np.arange(GLA_C)
    tri2 = jnp.asarray(np.stack([idx[None, :] <= idx[:, None], idx[None, :] >= idx[:, None]]), F32)

    assert B + 1 <= ADALN_ROWS
    cvec = jnp.concatenate([c, c_ctx[None, :], jnp.zeros((ADALN_ROWS - B - 1, D), F32)], axis=0)
    ada = _adaln(cvec, w_ada, b_ada)
    zero_state = jnp.zeros((2, B, GLA_HEADS, GLA_DK, GLA_DV), F32)

    p = _stacked_params(w_in, q_norm, k_norm, w_att_o, gla_w_a2, gla_b_a, gla_norm, w_gla_o, conv_w_dw,
                        conv_b_dw, conv_ln_g, conv_ln_b, w_conv_o, w_out, ln1_g, ln1_b, w_ff_gate,
                        w_ff_up, w_ff_down, ln2_g, ln2_b)
    xc = ctx
    for l in range(DEPTH):
        mod = ada[l, :B].reshape(B, 6, D)
        mod_c = jnp.broadcast_to(ada[l, B].reshape(1, 6, D), (B, 6, D))
        last = l == DEPTH - 1

        qc, kc, vc, gkc, gvc, gqc, rc, yc, gatesc, bcc = _inproj(xc, mod_c, tabs_ctx, p, l, rope=False,
                                                                 tm=tm_ctx)
        ogf_c, ogb_c, s_ctx = _gla(gqc, gkc, gvc, bcc, zero_state, tri2, tb=tm_ctx)
        if not last:
            o_att_c, conv_c = _attention(qc, [(kc, vc)], yc, p['conv_w_dw'], l, tq=tq_ctx,
                                         tk=_key_chunk(LC))
            xc = _merge(xc, mod_c, o_att_c, ogf_c, ogb_c, rc, conv_c, gatesc, p, l, tm=tm_ctx)
            xc = _ffn(xc, mod_c, p, l, tm=tm_ctx)

        q, k, v, gk, gv, gq, r, y, gates, bc = _inproj(x, mod, tabs_lat, p, l, rope=True, tm=tm_lat)
        o_att, conv = _attention(q, [(k, v), (kc, vc)], y, p['conv_w_dw'], l, tq=min(ATT_TQ, S),
                                 tk=_key_chunk(S + LC))
        ogf, ogb, _ = _gla(gq, gk, gv, bc, s_ctx, tri2, tb=tm_lat)
        x = _merge(x, mod, o_att, ogf, ogb, r, conv, gates, p, l, tm=tm_lat)
        x = _ffn(x, mod, p, l, tm=tm_lat)
    return x
```

```python
import functools

import numpy as np
import jax
import jax.numpy as jnp
from jax import lax
from jax.experimental import pallas as pl
from jax.experimental.pallas import tpu as pltpu

F32 = jnp.float32
BF16 = jnp.bfloat16

D_MODEL = 1024
DEPTH = 2
GRID_W = 64
N_HEADS = 8
N_KV_HEADS = 2
HEAD_DIM = 128
Q_PER_KV = N_HEADS // N_KV_HEADS
ROPE_THETA = 10000.0
GLA_HEADS = 4
GLA_DK = 128
GLA_DV = 256
GLA_RANK = 16
GLA_TAU = 16.0
GLA_C = 128
CONV_W = 31
CONV_PAD = CONV_W // 2
HALO = 16
SUBLANES = 8
D_FF = 2816
ALPHA = (2.0 * DEPTH) ** 0.25
NORM_EPS = 1e-6
ATT_Q = N_HEADS * HEAD_DIM
ATT_KV = N_KV_HEADS * HEAD_DIM
V_ROWS = HEAD_DIM + 16
GLA_K = GLA_HEADS * GLA_DK
GLA_V = GLA_HEADS * GLA_DV
DECAY_SAFE = 80.0
LOG2E = 1.4426950408889634
L_MIN = 1e-20

C_K = 0
C_GK = C_K + 2 * ATT_KV
C_GV = C_GK + GLA_K
C_MEM_END = C_GV + GLA_V
C_Q = C_MEM_END
C_GQ = C_Q + ATT_Q
C_R = C_GQ + GLA_K
C_GLU_A = C_R + GLA_V
C_GLU_G = C_GLU_A + D_MODEL
C_GATES = C_GLU_G + D_MODEL
C_END = C_GATES + 3 * D_MODEL

VMEM_LIMIT = 56 * 1024 * 1024
MXU_TILE = 256

ROW_TILE = 512
GLA_TB = 1024
SUB_TILE = 256
ATT_TQ = 256
PROJ_COLS = 2 * MXU_TILE
FFN_COLS = 4 * MXU_TILE
ADALN_COLS = 6 * MXU_TILE
ADALN_ROWS = SUBLANES
CONV_ROWS = 32


def _cparams(sem):
    return pltpu.CompilerParams(dimension_semantics=sem, vmem_limit_bytes=VMEM_LIMIT)


def _resident(shape):
    nd = len(shape)
    return pl.BlockSpec(shape, lambda *_: (0,) * nd, pipeline_mode=pl.Buffered(1))


def _layer_resident(shape, l):
    nd = len(shape)
    return pl.BlockSpec((None,) + tuple(shape), lambda *_: (l,) + (0,) * nd, pipeline_mode=pl.Buffered(1))


def _ln(x):
    mu = jnp.mean(x, axis=-1, keepdims=True)
    xc = x - mu
    var = jnp.mean(xc * xc, axis=-1, keepdims=True)
    return xc * lax.rsqrt(var + NORM_EPS)


def _silu(x):
    return x * jax.nn.sigmoid(x)


def _dot(a, b):
    return jnp.dot(a, b, preferred_element_type=F32)


def _dot_nt(a, b):
    return lax.dot_general(a, b, (((1,), (1,)), ((), ())), preferred_element_type=F32)


def _dot_tn(a, b):
    return lax.dot_general(a, b, (((0,), (0,)), ((), ())), preferred_element_type=F32)


def _adaln_kernel(c_ref, w_ref, b_ref, o_ref):
    s = _silu(c_ref[...])
    o_ref[0] = jnp.dot(s, w_ref[0], preferred_element_type=F32,
                       precision=lax.Precision.HIGHEST) + b_ref[0]


def _adaln(cvec, w_ada, b_ada):
    L, D, N6 = w_ada.shape
    tn = ADALN_COLS
    return pl.pallas_call(
        _adaln_kernel,
        out_shape=jax.ShapeDtypeStruct((L, ADALN_ROWS, N6), F32),
        grid=(L, N6 // tn),
        in_specs=[pl.BlockSpec((ADALN_ROWS, D), lambda l, n: (0, 0)),
                  pl.BlockSpec((1, D, tn), lambda l, n: (l, 0, n)),
                  pl.BlockSpec((1, 1, tn), lambda l, n: (l, 0, n))],
        out_specs=pl.BlockSpec((1, ADALN_ROWS, tn), lambda l, n: (l, 0, n)),
        compiler_params=_cparams(("arbitrary", "arbitrary")),
        name="adaln",
    )(cvec, w_ada, b_ada.reshape(L, 1, N6))


def _inproj_kernel(x_ref, mod_ref, cos_ref, sin_ref, qn_ref, kn_ref, ba_ref, wmem_ref, wrest_ref, wvt_ref,
                   wglr_ref, w2_ref, tril_ref, triu_ref,
                   q_o, k_o, vt_o, gk_o, gv_o, gq_o, r_o, y_o, gates_o, bc_o, h_scr, *, rope):
    tm = tril_ref.shape[0]
    for r0 in range(0, h_scr.shape[0], tm):
        _inproj_rows(slice(r0, r0 + tm), tm, rope, x_ref, mod_ref, cos_ref, sin_ref, qn_ref, kn_ref, ba_ref,
                     (wmem_ref, wrest_ref), wvt_ref, wglr_ref, w2_ref, tril_ref, triu_ref,
                     q_o, k_o, vt_o, gk_o, gv_o, gq_o, r_o, y_o, gates_o, bc_o, h_scr)


def _inproj_rows(rows, tm, rope, x_ref, mod_ref, cos_ref, sin_ref, qn_ref, kn_ref, ba_ref, w_ref, wvt_ref,
                 wglr_ref, w2_ref, tril_ref, triu_ref,
                 q_o, k_o, vt_o, gk_o, gv_o, gq_o, r_o, y_o, gates_o, bc_o, h_scr):
    shift = mod_ref[0, 0:1, :]
    scale = mod_ref[0, 1:2, :]
    h_scr[rows, :] = (_ln(x_ref[0, rows, :]) * (1.0 + scale) + shift).astype(BF16)

    def mm(c0, width):
        ref, base = (w_ref[0], 0) if c0 < C_MEM_END else (w_ref[1], C_MEM_END)
        return _dot(h_scr[rows, :], ref[:, c0 - base:c0 - base + width])

    even_lane = lax.broadcasted_iota(jnp.int32, (tm, HEAD_DIM), 1) % 2 == 0

    def norm_rope(xh, gain):
        y = xh * lax.rsqrt(jnp.mean(xh * xh, axis=-1, keepdims=True) + NORM_EPS) * gain
        if rope:
            partner = jnp.where(even_lane, pltpu.roll(y, HEAD_DIM - 1, 1), pltpu.roll(y, 1, 1))
            y = y * cos_ref[rows, :] + partner * sin_ref[rows, :]
        return y

    for c in range(ATT_Q // PROJ_COLS):
        acc = mm(C_Q + c * PROJ_COLS, PROJ_COLS)
        for g in range(4):
            y = norm_rope(acc[:, g * 128:(g + 1) * 128], qn_ref[...]) * (LOG2E * HEAD_DIM ** -0.5)
            q_o[0, rows, c * PROJ_COLS + g * 128:c * PROJ_COLS + (g + 1) * 128] = y.astype(BF16)
    acc = mm(C_K, ATT_KV)
    for g in range(N_KV_HEADS):
        y = norm_rope(acc[:, g * 128:(g + 1) * 128], kn_ref[...])
        k_o[0, rows, g * 128:(g + 1) * 128] = y.astype(BF16)
    vt = _dot_nt(wvt_ref[...], h_scr[rows, :])
    for g in range(N_KV_HEADS):
        vt_o[0, g * V_ROWS:g * V_ROWS + HEAD_DIM, rows] = vt[g * 128:(g + 1) * 128].astype(BF16)
        vt_o[0, g * V_ROWS + HEAD_DIM:(g + 1) * V_ROWS, rows] = jnp.ones((V_ROWS - HEAD_DIM, tm), BF16)

    gk_o[0, rows, :] = mm(C_GK, GLA_K).astype(BF16)
    for c in range(GLA_V // PROJ_COLS):
        gv_o[0, rows, c * PROJ_COLS:(c + 1) * PROJ_COLS] = mm(C_GV + c * PROJ_COLS, PROJ_COLS).astype(BF16)
    gq_o[0, rows, :] = (mm(C_GQ, GLA_K) * (GLA_DK ** -0.5)).astype(BF16)
    for c in range(GLA_V // PROJ_COLS):
        r_o[0, rows, c * PROJ_COLS:(c + 1) * PROJ_COLS] = _silu(mm(C_R + c * PROJ_COLS, PROJ_COLS)).astype(BF16)
    for c in range(D_MODEL // PROJ_COLS):
        a = mm(C_GLU_A + c * PROJ_COLS, PROJ_COLS)
        g = mm(C_GLU_G + c * PROJ_COLS, PROJ_COLS)
        y_o[0, rows, c * PROJ_COLS:(c + 1) * PROJ_COLS] = (a * jax.nn.sigmoid(g)).astype(BF16)
    for c in range(3 * D_MODEL // PROJ_COLS):
        gates_o[0, rows, c * PROJ_COLS:(c + 1) * PROJ_COLS] = jax.nn.sigmoid(mm(C_GATES + c * PROJ_COLS, PROJ_COLS)).astype(BF16)

    glr = _dot(h_scr[rows, :], wglr_ref[...])
    z = _dot(glr.astype(BF16), w2_ref[...]) + ba_ref[...]
    lg = (jnp.minimum(z, 0.0) - jnp.log(1.0 + jnp.exp(-jnp.abs(z)))) * (1.0 / GLA_TAU)
    for dr, tri_ref in ((0, tril_ref), (1, triu_ref)):
        g = lg[:, dr * GLA_K:(dr + 1) * GLA_K]
        g1 = g.astype(BF16)
        r1 = g - g1.astype(F32)
        g2 = r1.astype(BF16)
        tri = tri_ref[...]
        bc_o[dr, 0, rows, :] = _dot(tri, g1) + _dot(tri, g2)


def _inproj(x, mod, tabs, p, l, *, rope, tm):
    B, N, D = x.shape
    res = lambda shape: _layer_resident(shape, l)
    nt = N // tm
    row = lambda w: pl.BlockSpec((1, tm, w), lambda b, i: (b, i, 0))
    widths = (ATT_Q, ATT_KV, GLA_K, GLA_V, GLA_K, GLA_V, D_MODEL, 3 * D_MODEL)
    out_shapes = [jax.ShapeDtypeStruct((B, N, w), BF16) for w in widths]
    out_specs = [row(w) for w in widths]
    out_shapes.insert(2, jax.ShapeDtypeStruct((B, N_KV_HEADS * V_ROWS, N), BF16))
    out_specs.insert(2, pl.BlockSpec((1, N_KV_HEADS * V_ROWS, tm), lambda b, i: (b, 0, i)))
    out_shapes.append(jax.ShapeDtypeStruct((2, B, N, GLA_K), F32))
    out_specs.append(pl.BlockSpec((2, 1, tm, GLA_K), lambda b, i: (0, b, i, 0)))
    sub = min(SUB_TILE, tm)
    tril, triu = tabs['tri'][sub]
    return pl.pallas_call(
        functools.partial(_inproj_kernel, rope=rope),
        out_shape=out_shapes,
        grid=(B, nt),
        in_specs=[row(D),
                  pl.BlockSpec((1, 6, D), lambda b, i: (b, 0, 0)),
                  pl.BlockSpec((tm, HEAD_DIM), lambda b, i: (i, 0)),
                  pl.BlockSpec((tm, HEAD_DIM), lambda b, i: (i, 0)),
                  res((1, HEAD_DIM)), res((1, HEAD_DIM)), res((1, 2 * GLA_K)),
                  res((D, C_MEM_END)), res((D, C_END - C_MEM_END)), res((ATT_KV, D)), res((D, 2 * GLA_RANK)),
                  res((2 * GLA_RANK, 2 * GLA_K)),
                  _resident((sub, sub)), _resident((sub, sub))],
        out_specs=out_specs,
        scratch_shapes=[pltpu.VMEM((tm, D), BF16)],
        compiler_params=_cparams(("arbitrary", "arbitrary")),
        name="in_proj_rope" if rope else "in_proj",
    )(x, mod, tabs['cos'], tabs['sin'], p['q_norm'], p['k_norm'], p['b_a'], p['w_mem'], p['w_rest'],
      p['w_vt'], p['w_glr'], p['w_a2'], tril, triu)


def _dwconv(y_ref, yp_ref, yn_ref, w_ref, o_ref, ybuf, ysh, first, last):
    n = y_ref.shape[1]
    span = ysh.shape[1]

    def fill():
        ybuf[0:HALO, :] = jnp.where(first, 0.0, yp_ref[0].astype(F32))
        ybuf[HALO:HALO + n, :] = y_ref[0].astype(F32)
        ybuf[HALO + n:, :] = jnp.where(last, 0.0, yn_ref[0].astype(F32))

    def lane_block(cb, start_zero):
        lanes = slice(cb * 128, (cb + 1) * 128)
        for s in range(1, SUBLANES):
            ysh[s - 1, :, lanes] = ybuf[s:s + span, lanes]
        acc = jnp.broadcast_to(start_zero, (n, 128))
        for t in range(CONV_W):
            a, s = divmod(HALO - CONV_PAD + t, SUBLANES)
            rows = slice(a * SUBLANES, a * SUBLANES + n)
            src = ybuf[rows, lanes] if s == 0 else ysh[s - 1, rows, lanes]
            acc = acc + src * w_ref[t:t + 1, lanes]
        o_ref[0, :, lanes] = acc
        return _zero_after(functools.reduce(jnp.maximum, [acc[r:r + SUBLANES] for r in range(0, n, SUBLANES)]))

    return fill, [functools.partial(lane_block, cb) for cb in range(D_MODEL // 128)]


def _zero_after(x):
    return ((pltpu.bitcast(x, jnp.uint32) >> 16) >> 16).astype(F32)


def _attn_kernel(*refs, src_len, tq, tk):
    n_src = len(src_len)
    q_ref = refs[0]
    src_refs = refs[1:1 + 2 * n_src]
    y_ref, yp_ref, yn_ref, wdw_ref, o_ref, conv_ref = refs[1 + 2 * n_src:7 + 2 * n_src]
    k_scr, vt_scr, kmax_scr, acc_scr, m_scr, ybuf, ysh = refs[7 + 2 * n_src:]
    M = Q_PER_KV * tq
    n_chunk = sum(src_len) // tk
    conv_blk = pl.program_id(1) * pl.num_programs(2) + pl.program_id(2)
    n_conv_blk = pl.num_programs(1) * pl.num_programs(2)

    def rolled(fn):
        def body(i, carry):
            rows = pl.ds(pl.multiple_of(i * tk, tk), tk)
            fn(k_scr[rows, :], vt_scr[:, rows])
            return carry
        lax.fori_loop(0, n_chunk, body, 0)

    @pl.when(pl.program_id(2) == 0)
    def _():
        off = 0
        for s_i, n in enumerate(src_len):
            k_scr[off:off + n, :] = src_refs[2 * s_i][0]
            vt_scr[:, off:off + n] = src_refs[2 * s_i + 1][0]
            off += n
        kmax_scr[...] = jnp.zeros(kmax_scr.shape, F32)

        def knorm(k, vt):
            kf = k.astype(F32)
            n2 = jnp.max(jnp.sum(kf * kf, axis=1, keepdims=True), axis=0, keepdims=True)
            kmax_scr[...] = jnp.maximum(kmax_scr[...], n2)
        rolled(knorm)

    q4 = jnp.concatenate([q_ref[0, :, g * 128:(g + 1) * 128] for g in range(Q_PER_KV)], axis=0)
    qf = q4.astype(F32)
    qn2 = _dot_nt(jnp.ones((SUBLANES, HEAD_DIM), BF16), (qf * qf).astype(BF16))[0:1, :]
    shift = jnp.sqrt(qn2 * kmax_scr[0:1, 0:1])

    conv_fill, conv_pieces = _dwconv(y_ref, yp_ref, yn_ref, wdw_ref, conv_ref, ybuf, ysh,
                                     conv_blk == 0, conv_blk == n_conv_blk - 1)
    conv_fill()
    per_chunk = -(-len(conv_pieces) // max(n_chunk - 2, 1))

    acc = jnp.zeros((V_ROWS, M), F32)
    piece_zeros = {}
    for c in range(n_chunk):
        shift_c = shift
        for zero in piece_zeros.pop(c, []):
            shift_c = jnp.concatenate([shift_c[:, :128] + zero[0:1, :], shift_c[:, 128:]], axis=1)
        rows = slice(c * tk, (c + 1) * tk)
        st = _dot_nt(k_scr[rows, :], q4)
        acc = acc + _dot(vt_scr[:, rows], jnp.exp2(st - shift_c).astype(BF16))
        for piece in conv_pieces[c * per_chunk:(c + 1) * per_chunk]:
            zero = piece(_zero_after(acc[0:1, 0:128]))
            if c + 3 < n_chunk:
                piece_zeros.setdefault(c + 3, []).append(zero)
    acc_scr[...] = acc

    def write_out():
        a = acc_scr[...]
        ot = a[:HEAD_DIM, :] / a[HEAD_DIM:HEAD_DIM + 1, :]
        for g in range(Q_PER_KV):
            o_ref[0, :, g * 128:(g + 1) * 128] = jnp.transpose(ot[:, g * tq:(g + 1) * tq]).astype(BF16)
    write_out()

    l_min = jnp.min(acc_scr[HEAD_DIM:HEAD_DIM + 1, :])

    @pl.when(jnp.logical_not(l_min >= L_MIN))
    def _():
        m_scr[...] = jnp.full(m_scr.shape, -jnp.inf, F32)
        acc_scr[...] = jnp.zeros(acc_scr.shape, F32)

        def online(k, vt):
            st = _dot_nt(k, q4)
            m_prev = m_scr[0:1, :]
            m_new = jnp.maximum(m_prev, jnp.max(st, axis=0, keepdims=True))
            alpha = jnp.exp2(m_prev - m_new)
            acc_scr[...] = alpha * acc_scr[...] + _dot(vt, jnp.exp2(st - m_new).astype(BF16))
            m_scr[...] = jnp.broadcast_to(m_new, m_scr.shape)
        rolled(online)
        write_out()


def _attention(q, srcs, y, w_dw, l, *, tq, tk):
    B, Nq, D = y.shape
    src_len = tuple(k.shape[1] for k, _ in srcs)
    lk = sum(src_len)
    assert lk % tk == 0
    M = Q_PER_KV * tq
    nq = Nq // tq
    cr = tq // N_KV_HEADS
    hb = cr // HALO
    n_halo = Nq // HALO
    conv_blk = lambda h, i: h * nq + i
    in_specs = [pl.BlockSpec((1, tq, Q_PER_KV * HEAD_DIM), lambda b, h, i: (b, i, h))]
    args = [q]
    for (k, vt), n in zip(srcs, src_len):
        in_specs += [pl.BlockSpec((1, n, HEAD_DIM), lambda b, h, i: (b, 0, h)),
                     pl.BlockSpec((1, V_ROWS, n), lambda b, h, i: (b, h, 0))]
        args += [k, vt]
    in_specs += [pl.BlockSpec((1, cr, D), lambda b, h, i: (b, conv_blk(h, i), 0)),
                 pl.BlockSpec((1, HALO, D), lambda b, h, i: (b, jnp.maximum(conv_blk(h, i) * hb - 1, 0), 0)),
                 pl.BlockSpec((1, HALO, D),
                              lambda b, h, i: (b, jnp.minimum((conv_blk(h, i) + 1) * hb, n_halo - 1), 0)),
                 _layer_resident((CONV_ROWS, D), l)]
    args += [y, y, y, w_dw]
    return pl.pallas_call(
        functools.partial(_attn_kernel, src_len=src_len, tq=tq, tk=tk),
        out_shape=[jax.ShapeDtypeStruct((B, Nq, ATT_Q), BF16), jax.ShapeDtypeStruct((B, Nq, D), F32)],
        grid=(B, N_KV_HEADS, nq),
        in_specs=in_specs,
        out_specs=[pl.BlockSpec((1, tq, Q_PER_KV * HEAD_DIM), lambda b, h, i: (b, i, h)),
                   pl.BlockSpec((1, cr, D), lambda b, h, i: (b, conv_blk(h, i), 0))],
        scratch_shapes=[pltpu.VMEM((lk, HEAD_DIM), BF16), pltpu.VMEM((V_ROWS, lk), BF16),
                        pltpu.VMEM((SUBLANES, 128), F32), pltpu.VMEM((V_ROWS, M), F32),
                        pltpu.VMEM((SUBLANES, M), F32),
                        pltpu.VMEM((cr + 2 * HALO, D), F32),
                        pltpu.VMEM((SUBLANES - 1, cr + 2 * HALO - SUBLANES, D), F32)],
        compiler_params=_cparams(("arbitrary", "arbitrary", "arbitrary")),
        name="attn%d" % len(srcs),
    )(*args)


def _gla_kernel(qf_ref, kf_ref, vf_ref, bf_ref, qb_ref, kb_ref, vb_ref, bb_ref, tri_ref, s0_ref,
                of_ref, ob_ref, send_ref, s_scr, *, n_chunk):
    j = pl.program_id(1)

    @pl.when(j == 0)
    def _():
        s_scr[...] = s0_ref[:, 0]

    dirs = ((qf_ref, kf_ref, vf_ref, bf_ref, of_ref), (qb_ref, kb_ref, vb_ref, bb_ref, ob_ref))

    def intra_exact(q, k, b):
        row_id = lax.broadcasted_iota(jnp.int32, (GLA_C, GLA_DK), 0)
        col_id = lax.broadcasted_iota(jnp.int32, (GLA_C, GLA_C), 1)

        def one(jrow, a):
            pick = row_id == jrow
            k_j = jnp.sum(jnp.where(pick, k, 0.0), axis=0, keepdims=True)
            b_j = jnp.sum(jnp.where(pick, b, 0.0), axis=0, keepdims=True)
            col = jnp.sum(q * k_j * jnp.exp(jnp.minimum(b - b_j, 0.0)), axis=1, keepdims=True)
            return jnp.where(col_id == jrow, col, a)
        return lax.fori_loop(0, GLA_C, one, jnp.zeros((GLA_C, GLA_C), F32))

    def make_body(exact):
        def body(i, carry):
            for d, (q_ref, k_ref, v_ref, b_ref, o_ref) in enumerate(dirs):
                c = i if d == 0 else n_chunk - 1 - i
                rows = pl.ds(pl.multiple_of(c * GLA_C, GLA_C), GLA_C)
                tri = tri_ref[d]
                for h in range(GLA_HEADS):
                    kc = slice(h * GLA_DK, (h + 1) * GLA_DK)
                    vc = slice(h * GLA_DV, (h + 1) * GLA_DV)
                    b = b_ref[0, 0, rows, kc]
                    q = q_ref[0, rows, kc].astype(F32)
                    k = k_ref[0, rows, kc].astype(F32)
                    v = v_ref[0, rows, vc]
                    b_end = jnp.min(b, axis=0, keepdims=True)
                    qe = q * jnp.exp(b)
                    kh = k * jnp.exp(b_end - b)
                    if exact:
                        a = intra_exact(q, k, b)
                    else:
                        mid = jnp.exp(-0.5 * b_end)
                        a = _dot_nt((qe * mid).astype(BF16), (kh * mid).astype(BF16))
                    qe = qe.astype(BF16)
                    a = a * tri
                    av = _dot(jnp.concatenate([a.astype(BF16), jnp.transpose(kh).astype(BF16)], axis=0), v)
                    s = s_scr[d, h]
                    o_ref[0, rows, vc] = (_dot(qe, s.astype(BF16)) + av[:GLA_C]).astype(BF16)
                    e_col = jnp.transpose(jnp.broadcast_to(jnp.exp(b_end), (GLA_DK, GLA_DK)))
                    s_scr[d, h] = jnp.concatenate([e_col, e_col], axis=1) * s + av[GLA_C:]
            return carry
        return body

    safe = jnp.minimum(jnp.min(bf_ref[...]), jnp.min(bb_ref[...])) >= -DECAY_SAFE

    @pl.when(safe)
    def _():
        lax.fori_loop(0, n_chunk, make_body(False), 0, unroll=2)

    @pl.when(jnp.logical_not(safe))
    def _():
        lax.fori_loop(0, n_chunk, make_body(True), 0)

    @pl.when(j == pl.num_programs(1) - 1)
    def _():
        send_ref[:, 0] = s_scr[...]


def _gla(gq, gk, gv, bc, s0, tri2, *, tb):
    B, N, _ = gq.shape
    nb = N // tb
    fwd = lambda w: pl.BlockSpec((1, tb, w), lambda b, j: (b, j, 0))
    bwd = lambda w: pl.BlockSpec((1, tb, w), lambda b, j: (b, nb - 1 - j, 0))
    state = pl.BlockSpec((2, 1, GLA_HEADS, GLA_DK, GLA_DV), lambda b, j: (0, b, 0, 0, 0))
    return pl.pallas_call(
        functools.partial(_gla_kernel, n_chunk=tb // GLA_C),
        out_shape=[jax.ShapeDtypeStruct((B, N, GLA_V), BF16), jax.ShapeDtypeStruct((B, N, GLA_V), BF16),
                   jax.ShapeDtypeStruct((2, B, GLA_HEADS, GLA_DK, GLA_DV), F32)],
        grid=(B, nb),
        in_specs=[fwd(GLA_K), fwd(GLA_K), fwd(GLA_V),
                  pl.BlockSpec((1, 1, tb, GLA_K), lambda b, j: (0, b, j, 0)),
                  bwd(GLA_K), bwd(GLA_K), bwd(GLA_V),
                  pl.BlockSpec((1, 1, tb, GLA_K), lambda b, j: (1, b, nb - 1 - j, 0)),
                  _resident((2, GLA_C, GLA_C)), state],
        out_specs=[fwd(GLA_V), bwd(GLA_V), state],
        scratch_shapes=[pltpu.VMEM((2, GLA_HEADS, GLA_DK, GLA_DV), F32)],
        compiler_params=_cparams(("arbitrary", "arbitrary")),
        name="gla",
    )(gq, gk, gv, bc, gq, gk, gv, bc, tri2, s0)


def _merge_kernel(x_ref, mod_ref, oatt_ref, ogf_ref, ogb_ref, r_ref, conv_ref, gates_ref,
                  gn_ref, bdw_ref, clg_ref, clb_ref, watt_ref, wgla_ref, wconv_ref, wout_ref,
                  lng_ref, lnb_ref, xo_ref, *, sub):
    for r0 in range(0, x_ref.shape[1], sub):
        rows = slice(r0, r0 + sub)
        yc = _ln(conv_ref[0, rows, :] + bdw_ref[...]) * clg_ref[...] + clb_ref[...]
        y_conv = _dot(_silu(yc).astype(BF16), wconv_ref[...])

        y_att = _dot(oatt_ref[0, rows, :], watt_ref[...])

        parts = []
        for h in range(GLA_HEADS):
            vc = slice(h * GLA_DV, (h + 1) * GLA_DV)
            o = ogf_ref[0, rows, vc].astype(F32) + ogb_ref[0, rows, vc].astype(F32)
            o = o * lax.rsqrt(jnp.mean(o * o, axis=-1, keepdims=True) + NORM_EPS) * gn_ref[...]
            parts.append((o * r_ref[0, rows, vc].astype(F32)).astype(BF16))
        y_gla = _dot(jnp.concatenate(parts, axis=1), wgla_ref[...])

        g_att = gates_ref[0, rows, 0:D_MODEL].astype(F32)
        g_gla = gates_ref[0, rows, D_MODEL:2 * D_MODEL].astype(F32)
        g_conv = gates_ref[0, rows, 2 * D_MODEL:].astype(F32)
        mix = g_att * y_att + g_gla * y_gla + g_conv * y_conv
        f = _dot(mix.astype(BF16), wout_ref[...])
        gate = mod_ref[0, 2:3, :]
        xo_ref[0, rows, :] = _ln(ALPHA * x_ref[0, rows, :] + gate * f) * lng_ref[...] + lnb_ref[...]


def _merge(x, mod, o_att, o_gla_f, o_gla_b, r, conv, gates, p, l, *, tm):
    B, N, D = x.shape
    row = lambda w: pl.BlockSpec((1, tm, w), lambda b, i: (b, i, 0))
    res = lambda shape: _layer_resident(shape, l)
    vec = res((1, D))
    return pl.pallas_call(
        functools.partial(_merge_kernel, sub=min(SUB_TILE, tm)),
        out_shape=jax.ShapeDtypeStruct((B, N, D), F32),
        grid=(B, N // tm),
        in_specs=[row(D),
                  pl.BlockSpec((1, 6, D), lambda b, i: (b, 0, 0)),
                  row(ATT_Q), row(GLA_V), row(GLA_V), row(GLA_V), row(D), row(3 * D),
                  res((1, GLA_DV)), vec, vec, vec,
                  res((ATT_Q, D)), res((GLA_V, D)), res((D, D)), res((D, D)),
                  vec, vec],
        out_specs=row(D),
        compiler_params=_cparams(("arbitrary", "arbitrary")),
        name="merge",
    )(x, mod, o_att, o_gla_f, o_gla_b, r, conv, gates,
      p['gla_norm'], p['conv_b_dw'], p['conv_ln_g'], p['conv_ln_b'],
      p['w_att_o'], p['w_gla_o'], p['w_conv_o'], p['w_out'], p['ln1_g'], p['ln1_b'])


def _ffn_kernel(x_ref, mod_ref, wg_ref, wu_ref, wd_ref, lng_ref, lnb_ref, xo_ref, h_scr, a_scr, *, tf, sub):
    for r0 in range(0, x_ref.shape[1], sub):
        rows = slice(r0, r0 + sub)
        x = x_ref[0, rows, :]
        h_scr[rows, :] = (_ln(x) * (1.0 + mod_ref[0, 4:5, :]) + mod_ref[0, 3:4, :]).astype(BF16)
        for c0 in range(0, D_FF, tf):
            cols = slice(c0, min(c0 + tf, D_FF))
            g = _dot(h_scr[rows, :], wg_ref[:, cols])
            u = _dot(h_scr[rows, :], wu_ref[:, cols])
            a_scr[rows, cols] = (_silu(g) * u).astype(BF16)
        f = _dot(a_scr[rows, :], wd_ref[...])
        xo_ref[0, rows, :] = _ln(ALPHA * x + mod_ref[0, 5:6, :] * f) * lng_ref[...] + lnb_ref[...]


def _ffn(x, mod, p, l, *, tm):
    B, N, D = x.shape
    row = pl.BlockSpec((1, tm, D), lambda b, i: (b, i, 0))
    res = lambda shape: _layer_resident(shape, l)
    return pl.pallas_call(
        functools.partial(_ffn_kernel, tf=FFN_COLS, sub=min(SUB_TILE, tm)),
        out_shape=jax.ShapeDtypeStruct((B, N, D), F32),
        grid=(B, N // tm),
        in_specs=[row, pl.BlockSpec((1, 6, D), lambda b, i: (b, 0, 0)),
                  res((D, D_FF)), res((D, D_FF)), res((D_FF, D)), res((1, D)), res((1, D))],
        out_specs=row,
        scratch_shapes=[pltpu.VMEM((tm, D), BF16), pltpu.VMEM((tm, D_FF), BF16)],
        compiler_params=_cparams(("arbitrary", "arbitrary")),
        name="ffn",
    )(x, mod, p['w_ff_gate'], p['w_ff_up'], p['w_ff_down'], p['ln2_g'], p['ln2_b'])


def _tri_blocks(tm):
    idx = np.arange(tm)
    same = (idx[:, None] // GLA_C) == (idx[None, :] // GLA_C)
    tril = same & (idx[None, :] <= idx[:, None])
    triu = same & (idx[None, :] >= idx[:, None])
    return jnp.asarray(tril, BF16), jnp.asarray(triu, BF16)


def _rope_tables(S):
    t = np.arange(S)
    row = (t // GRID_W).astype(np.float32)
    col = (t % GRID_W).astype(np.float32)
    half = HEAD_DIM // 2
    inv = (ROPE_THETA ** (-np.arange(0, half, 2, dtype=np.float32) / half)).astype(np.float32)
    ang = np.concatenate([row[:, None] * inv, col[:, None] * inv], axis=-1).astype(np.float32)
    cos, sin = np.cos(ang), np.sin(ang)
    return (jnp.asarray(np.repeat(cos, 2, axis=-1), F32),
            jnp.asarray(np.stack([-sin, sin], axis=-1).reshape(S, HEAD_DIM), F32))


def _stacked_params(w_in, q_norm, k_norm, w_att_o, gla_w_a2, gla_b_a, gla_norm, w_gla_o, conv_w_dw,
                    conv_b_dw, conv_ln_g, conv_ln_b, w_conv_o, w_out, ln1_g, ln1_b, w_ff_gate, w_ff_up,
                    w_ff_down, ln2_g, ln2_b):
    L = w_in.shape[0]
    glr0 = C_MEM_END
    glr1 = glr0 + 2 * GLA_RANK
    v_w = w_in[:, :, ATT_KV:2 * ATT_KV]
    zero = jnp.zeros((L, GLA_RANK, GLA_K), F32)
    w_a2 = jnp.concatenate([jnp.concatenate([gla_w_a2[:, 0], zero], axis=2),
                            jnp.concatenate([zero, gla_w_a2[:, 1]], axis=2)], axis=1).astype(BF16)
    vec = lambda a: a.reshape(L, 1, -1)
    return {
        'w_mem': w_in[:, :, :glr0].astype(BF16), 'w_rest': w_in[:, :, glr1:].astype(BF16),
        'w_vt': jnp.swapaxes(v_w, 1, 2).astype(BF16), 'w_glr': w_in[:, :, glr0:glr1].astype(BF16),
        'w_a2': w_a2, 'b_a': vec(gla_b_a), 'q_norm': vec(q_norm), 'k_norm': vec(k_norm),
        'w_att_o': w_att_o.astype(BF16), 'gla_norm': vec(gla_norm), 'w_gla_o': w_gla_o.astype(BF16),
        'conv_w_dw': jnp.pad(conv_w_dw[:, :, 0, :], ((0, 0), (0, CONV_ROWS - CONV_W), (0, 0))),
        'conv_b_dw': vec(conv_b_dw), 'conv_ln_g': vec(conv_ln_g), 'conv_ln_b': vec(conv_ln_b),
        'w_conv_o': w_conv_o.astype(BF16), 'w_out': w_out.astype(BF16),
        'ln1_g': vec(ln1_g), 'ln1_b': vec(ln1_b),
        'w_ff_gate': w_ff_gate.astype(BF16), 'w_ff_up': w_ff_up.astype(BF16),
        'w_ff_down': w_ff_down.astype(BF16), 'ln2_g': vec(ln2_g), 'ln2_b': vec(ln2_b),
    }


def _key_chunk(n_keys):
    return max(t for t in range(MXU_TILE, 5 * MXU_TILE, MXU_TILE) if n_keys % t == 0)


def kernel(x, c, ctx, c_ctx, w_ada, b_ada, w_in, q_norm, k_norm, w_att_o, gla_w_a2, gla_b_a, gla_norm,
           w_gla_o, conv_w_dw, conv_b_dw, conv_ln_g, conv_ln_b, w_conv_o, w_out, ln1_g, ln1_b,
           w_ff_gate, w_ff_up, w_ff_down, ln2_g, ln2_b):
    B, S, D = x.shape
    LC = ctx.shape[1]
    tm_lat = min(ROW_TILE, S)
    tm_ctx = min(SUB_TILE, LC)
    tq_ctx = min(ATT_TQ // 2, LC)
    cos, sin = _rope_tables(S)
    tri_sizes = sorted({min(SUB_TILE, tm_lat), min(SUB_TILE, tm_ctx)})
    tabs_lat = {'cos': cos, 'sin': sin, 'tri': {t: _tri_blocks(t) for t in tri_sizes}}
    tabs_ctx = {'cos': cos[:LC], 'sin': sin[:LC], 'tri': tabs_lat['tri']}
    idx = np.arange(GLA_C)
    tri2 = jnp.asarray(np.stack([idx[None, :] <= idx[:, None], idx[None, :] >= idx[:, None]]), F32)

    assert B + 1 <= ADALN_ROWS
    cvec = jnp.concatenate([c, c_ctx[None, :], jnp.zeros((ADALN_ROWS - B - 1, D), F32)], axis=0)
    ada = _adaln(cvec, w_ada, b_ada)
    zero_state = jnp.zeros((2, B, GLA_HEADS, GLA_DK, GLA_DV), F32)

    p = _stacked_params(w_in, q_norm, k_norm, w_att_o, gla_w_a2, gla_b_a, gla_norm, w_gla_o, conv_w_dw,
                        conv_b_dw, conv_ln_g, conv_ln_b, w_conv_o, w_out, ln1_g, ln1_b, w_ff_gate,
                        w_ff_up, w_ff_down, ln2_g, ln2_b)
    xc = ctx
    for l in range(DEPTH):
        mod = ada[l, :B].reshape(B, 6, D)
        mod_c = jnp.broadcast_to(ada[l, B].reshape(1, 6, D), (B, 6, D))
        last = l == DEPTH - 1

        qc, kc, vc, gkc, gvc, gqc, rc, yc, gatesc, bcc = _inproj(xc, mod_c, tabs_ctx, p, l, rope=False,
                                                                 tm=tm_ctx)
        ogf_c, ogb_c, s_ctx = _gla(gqc, gkc, gvc, bcc, zero_state, tri2, tb=tm_ctx)
        if not last:
            o_att_c, conv_c = _attention(qc, [(kc, vc)], yc, p['conv_w_dw'], l, tq=tq_ctx,
                                         tk=_key_chunk(LC))
            xc = _merge(xc, mod_c, o_att_c, ogf_c, ogb_c, rc, conv_c, gatesc, p, l, tm=tm_ctx)
            xc = _ffn(xc, mod_c, p, l, tm=tm_ctx)

        q, k, v, gk, gv, gq, r, y, gates, bc = _inproj(x, mod, tabs_lat, p, l, rope=True, tm=tm_lat)
        o_att, conv = _attention(q, [(k, v), (kc, vc)], y, p['conv_w_dw'], l, tq=min(ATT_TQ, S),
                                 tk=_key_chunk(S + LC))
        ogf, ogb, _ = _gla(gq, gk, gv, bc, s_ctx, tri2, tb=min(GLA_TB, S))
        x = _merge(x, mod, o_att, ogf, ogb, r, conv, gates, p, l, tm=tm_lat)
        x = _ffn(x, mod, p, l, tm=tm_lat)
    return x
```

```python
import functools

import numpy as np
import jax
import jax.numpy as jnp
from jax import lax
from jax.experimental import pallas as pl
from jax.experimental.pallas import tpu as pltpu

F32 = jnp.float32
BF16 = jnp.bfloat16

D_MODEL = 1024
DEPTH = 2
GRID_W = 64
N_HEADS = 8
N_KV_HEADS = 2
HEAD_DIM = 128
Q_PER_KV = N_HEADS // N_KV_HEADS
ROPE_THETA = 10000.0
GLA_HEADS = 4
GLA_DK = 128
GLA_DV = 256
GLA_RANK = 16
GLA_TAU = 16.0
GLA_C = 128
CONV_W = 31
CONV_PAD = CONV_W // 2
HALO = 16
SUBLANES = 8
D_FF = 2816
ALPHA = (2.0 * DEPTH) ** 0.25
NORM_EPS = 1e-6
ATT_Q = N_HEADS * HEAD_DIM
ATT_KV = N_KV_HEADS * HEAD_DIM
V_ROWS = HEAD_DIM + 16
GLA_K = GLA_HEADS * GLA_DK
GLA_V = GLA_HEADS * GLA_DV
DECAY_SAFE = 80.0
LOG2E = 1.4426950408889634
L_MIN = 1e-20

C_K = 0
C_GK = C_K + 2 * ATT_KV
C_GV = C_GK + GLA_K
C_MEM_END = C_GV + GLA_V
C_Q = C_MEM_END
C_GQ = C_Q + ATT_Q
C_R = C_GQ + GLA_K
C_GLU_A = C_R + GLA_V
C_GLU_G = C_GLU_A + D_MODEL
C_GATES = C_GLU_G + D_MODEL
C_END = C_GATES + 3 * D_MODEL

VMEM_LIMIT = 56 * 1024 * 1024
MXU_TILE = 256

ROW_TILE = 512
GLA_TB = 1024
SUB_TILE = 256
ATT_TQ = 256
PROJ_COLS = 2 * MXU_TILE
FFN_COLS = 4 * MXU_TILE
ADALN_COLS = 6 * MXU_TILE
ADALN_ROWS = SUBLANES
CONV_ROWS = 32
CONV_LAG = 3


def _cparams(sem):
    return pltpu.CompilerParams(dimension_semantics=sem, vmem_limit_bytes=VMEM_LIMIT)


def _resident(shape):
    nd = len(shape)
    return pl.BlockSpec(shape, lambda *_: (0,) * nd, pipeline_mode=pl.Buffered(1))


def _layer_resident(shape, l):
    nd = len(shape)
    return pl.BlockSpec((None,) + tuple(shape), lambda *_: (l,) + (0,) * nd, pipeline_mode=pl.Buffered(1))


def _ln(x):
    mu = jnp.mean(x, axis=-1, keepdims=True)
    xc = x - mu
    var = jnp.mean(xc * xc, axis=-1, keepdims=True)
    return xc * lax.rsqrt(var + NORM_EPS)


def _silu(x):
    return x * jax.nn.sigmoid(x)


def _dot(a, b):
    return jnp.dot(a, b, preferred_element_type=F32)


def _dot_nt(a, b):
    return lax.dot_general(a, b, (((1,), (1,)), ((), ())), preferred_element_type=F32)


def _dot_tn(a, b):
    return lax.dot_general(a, b, (((0,), (0,)), ((), ())), preferred_element_type=F32)


def _adaln_kernel(c_ref, w_ref, b_ref, o_ref):
    s = _silu(c_ref[...])
    o_ref[0] = jnp.dot(s, w_ref[0], preferred_element_type=F32,
                       precision=lax.Precision.HIGHEST) + b_ref[0]


def _adaln(cvec, w_ada, b_ada):
    L, D, N6 = w_ada.shape
    tn = ADALN_COLS
    return pl.pallas_call(
        _adaln_kernel,
        out_shape=jax.ShapeDtypeStruct((L, ADALN_ROWS, N6), F32),
        grid=(L, N6 // tn),
        in_specs=[pl.BlockSpec((ADALN_ROWS, D), lambda l, n: (0, 0)),
                  pl.BlockSpec((1, D, tn), lambda l, n: (l, 0, n)),
                  pl.BlockSpec((1, 1, tn), lambda l, n: (l, 0, n))],
        out_specs=pl.BlockSpec((1, ADALN_ROWS, tn), lambda l, n: (l, 0, n)),
        compiler_params=_cparams(("arbitrary", "arbitrary")),
        name="adaln",
    )(cvec, w_ada, b_ada.reshape(L, 1, N6))


def _inproj_kernel(x_ref, mod_ref, cos_ref, sin_ref, qn_ref, kn_ref, ba_ref, wmem_ref, wrest_ref, wvt_ref,
                   wglr_ref, w2_ref, tril_ref, triu_ref,
                   q_o, k_o, vt_o, gk_o, gv_o, gq_o, r_o, y_o, gates_o, bc_o, h_scr, *, rope):
    tm = tril_ref.shape[0]
    for r0 in range(0, h_scr.shape[0], tm):
        _inproj_rows(slice(r0, r0 + tm), tm, rope, x_ref, mod_ref, cos_ref, sin_ref, qn_ref, kn_ref, ba_ref,
                     (wmem_ref, wrest_ref), wvt_ref, wglr_ref, w2_ref, tril_ref, triu_ref,
                     q_o, k_o, vt_o, gk_o, gv_o, gq_o, r_o, y_o, gates_o, bc_o, h_scr)


def _inproj_rows(rows, tm, rope, x_ref, mod_ref, cos_ref, sin_ref, qn_ref, kn_ref, ba_ref, w_ref, wvt_ref,
                 wglr_ref, w2_ref, tril_ref, triu_ref,
                 q_o, k_o, vt_o, gk_o, gv_o, gq_o, r_o, y_o, gates_o, bc_o, h_scr):
    shift = mod_ref[0, 0:1, :]
    scale = mod_ref[0, 1:2, :]
    h_scr[rows, :] = (_ln(x_ref[0, rows, :]) * (1.0 + scale) + shift).astype(BF16)

    def mm(c0, width):
        ref, base = (w_ref[0], 0) if c0 < C_MEM_END else (w_ref[1], C_MEM_END)
        return _dot(h_scr[rows, :], ref[:, c0 - base:c0 - base + width])

    even_lane = lax.broadcasted_iota(jnp.int32, (tm, HEAD_DIM), 1) % 2 == 0

    def norm_rope(xh, gain):
        y = xh * lax.rsqrt(jnp.mean(xh * xh, axis=-1, keepdims=True) + NORM_EPS) * gain
        if rope:
            partner = jnp.where(even_lane, pltpu.roll(y, HEAD_DIM - 1, 1), pltpu.roll(y, 1, 1))
            y = y * cos_ref[rows, :] + partner * sin_ref[rows, :]
        return y

    for c in range(ATT_Q // PROJ_COLS):
        acc = mm(C_Q + c * PROJ_COLS, PROJ_COLS)
        for g in range(4):
            y = norm_rope(acc[:, g * 128:(g + 1) * 128], qn_ref[...]) * (LOG2E * HEAD_DIM ** -0.5)
            q_o[0, rows, c * PROJ_COLS + g * 128:c * PROJ_COLS + (g + 1) * 128] = y.astype(BF16)
    acc = mm(C_K, ATT_KV)
    for g in range(N_KV_HEADS):
        y = norm_rope(acc[:, g * 128:(g + 1) * 128], kn_ref[...])
        k_o[0, rows, g * 128:(g + 1) * 128] = y.astype(BF16)
    vt = _dot_nt(wvt_ref[...], h_scr[rows, :])
    for g in range(N_KV_HEADS):
        vt_o[0, g * V_ROWS:g * V_ROWS + HEAD_DIM, rows] = vt[g * 128:(g + 1) * 128].astype(BF16)
        vt_o[0, g * V_ROWS + HEAD_DIM:(g + 1) * V_ROWS, rows] = jnp.ones((V_ROWS - HEAD_DIM, tm), BF16)

    gk_o[0, rows, :] = mm(C_GK, GLA_K).astype(BF16)
    for c in range(GLA_V // PROJ_COLS):
        gv_o[0, rows, c * PROJ_COLS:(c + 1) * PROJ_COLS] = mm(C_GV + c * PROJ_COLS, PROJ_COLS).astype(BF16)
    gq_o[0, rows, :] = (mm(C_GQ, GLA_K) * (GLA_DK ** -0.5)).astype(BF16)
    for c in range(GLA_V // PROJ_COLS):
        r_o[0, rows, c * PROJ_COLS:(c + 1) * PROJ_COLS] = _silu(mm(C_R + c * PROJ_COLS, PROJ_COLS)).astype(BF16)
    for c in range(D_MODEL // PROJ_COLS):
        a = mm(C_GLU_A + c * PROJ_COLS, PROJ_COLS)
        g = mm(C_GLU_G + c * PROJ_COLS, PROJ_COLS)
        y_o[0, rows, c * PROJ_COLS:(c + 1) * PROJ_COLS] = (a * jax.nn.sigmoid(g)).astype(BF16)
    for c in range(3 * D_MODEL // PROJ_COLS):
        gates_o[0, rows, c * PROJ_COLS:(c + 1) * PROJ_COLS] = jax.nn.sigmoid(mm(C_GATES + c * PROJ_COLS, PROJ_COLS)).astype(BF16)

    glr = _dot(h_scr[rows, :], wglr_ref[...])
    z = _dot(glr.astype(BF16), w2_ref[...]) + ba_ref[...]
    lg = (jnp.minimum(z, 0.0) - jnp.log(1.0 + jnp.exp(-jnp.abs(z)))) * (1.0 / GLA_TAU)
    for dr, tri_ref in ((0, tril_ref), (1, triu_ref)):
        g = lg[:, dr * GLA_K:(dr + 1) * GLA_K]
        g1 = g.astype(BF16)
        r1 = g - g1.astype(F32)
        g2 = r1.astype(BF16)
        tri = tri_ref[...]
        bc_o[dr, 0, rows, :] = _dot(tri, g1) + _dot(tri, g2)


def _inproj(x, mod, tabs, p, l, *, rope, tm):
    B, N, D = x.shape
    res = lambda shape: _layer_resident(shape, l)
    nt = N // tm
    row = lambda w: pl.BlockSpec((1, tm, w), lambda b, i: (b, i, 0))
    widths = (ATT_Q, ATT_KV, GLA_K, GLA_V, GLA_K, GLA_V, D_MODEL, 3 * D_MODEL)
    out_shapes = [jax.ShapeDtypeStruct((B, N, w), BF16) for w in widths]
    out_specs = [row(w) for w in widths]
    out_shapes.insert(2, jax.ShapeDtypeStruct((B, N_KV_HEADS * V_ROWS, N), BF16))
    out_specs.insert(2, pl.BlockSpec((1, N_KV_HEADS * V_ROWS, tm), lambda b, i: (b, 0, i)))
    out_shapes.append(jax.ShapeDtypeStruct((2, B, N, GLA_K), F32))
    out_specs.append(pl.BlockSpec((2, 1, tm, GLA_K), lambda b, i: (0, b, i, 0)))
    sub = min(SUB_TILE, tm)
    tril, triu = tabs['tri'][sub]
    return pl.pallas_call(
        functools.partial(_inproj_kernel, rope=rope),
        out_shape=out_shapes,
        grid=(B, nt),
        in_specs=[row(D),
                  pl.BlockSpec((1, 6, D), lambda b, i: (b, 0, 0)),
                  pl.BlockSpec((tm, HEAD_DIM), lambda b, i: (i, 0)),
                  pl.BlockSpec((tm, HEAD_DIM), lambda b, i: (i, 0)),
                  res((1, HEAD_DIM)), res((1, HEAD_DIM)), res((1, 2 * GLA_K)),
                  res((D, C_MEM_END)), res((D, C_END - C_MEM_END)), res((ATT_KV, D)), res((D, 2 * GLA_RANK)),
                  res((2 * GLA_RANK, 2 * GLA_K)),
                  _resident((sub, sub)), _resident((sub, sub))],
        out_specs=out_specs,
        scratch_shapes=[pltpu.VMEM((tm, D), BF16)],
        compiler_params=_cparams(("arbitrary", "arbitrary")),
        name="in_proj_rope" if rope else "in_proj",
    )(x, mod, tabs['cos'], tabs['sin'], p['q_norm'], p['k_norm'], p['b_a'], p['w_mem'], p['w_rest'],
      p['w_vt'], p['w_glr'], p['w_a2'], tril, triu)


def _dwconv(y_ref, yp_ref, yn_ref, w_ref, o_ref, ybuf, ysh, first, last):
    n = y_ref.shape[1]
    span = ysh.shape[1]

    def fill():
        ybuf[0:HALO, :] = jnp.where(first, 0.0, yp_ref[0].astype(F32))
        ybuf[HALO:HALO + n, :] = y_ref[0].astype(F32)
        ybuf[HALO + n:, :] = jnp.where(last, 0.0, yn_ref[0].astype(F32))

    def lane_block(cb, start_zero):
        lanes = slice(cb * 128, (cb + 1) * 128)
        for s in range(1, SUBLANES):
            ysh[s - 1, :, lanes] = ybuf[s:s + span, lanes]
        acc = jnp.broadcast_to(start_zero, (n, 128))
        for t in range(CONV_W):
            a, s = divmod(HALO - CONV_PAD + t, SUBLANES)
            rows = slice(a * SUBLANES, a * SUBLANES + n)
            src = ybuf[rows, lanes] if s == 0 else ysh[s - 1, rows, lanes]
            acc = acc + src * w_ref[t:t + 1, lanes]
        o_ref[0, :, lanes] = acc
        return _zero_after(functools.reduce(jnp.maximum, [acc[r:r + SUBLANES] for r in range(0, n, SUBLANES)]))

    return fill, [functools.partial(lane_block, cb) for cb in range(D_MODEL // 128)]


def _zero_after(x):
    return ((pltpu.bitcast(x, jnp.uint32) >> 16) >> 16).astype(F32)


def _attn_kernel(*refs, src_len, tq, tk):
    n_src = len(src_len)
    q_ref = refs[0]
    src_refs = refs[1:1 + 2 * n_src]
    y_ref, yp_ref, yn_ref, wdw_ref, o_ref, conv_ref = refs[1 + 2 * n_src:7 + 2 * n_src]
    k_scr, vt_scr, kmax_scr, acc_scr, m_scr, ybuf, ysh = refs[7 + 2 * n_src:]
    M = Q_PER_KV * tq
    n_chunk = sum(src_len) // tk
    conv_blk = pl.program_id(1) * pl.num_programs(2) + pl.program_id(2)
    n_conv_blk = pl.num_programs(1) * pl.num_programs(2)

    def rolled(fn):
        def body(i, carry):
            rows = pl.ds(pl.multiple_of(i * tk, tk), tk)
            fn(k_scr[rows, :], vt_scr[:, rows])
            return carry
        lax.fori_loop(0, n_chunk, body, 0)

    @pl.when(pl.program_id(2) == 0)
    def _():
        off = 0
        for s_i, n in enumerate(src_len):
            k_scr[off:off + n, :] = src_refs[2 * s_i][0]
            vt_scr[:, off:off + n] = src_refs[2 * s_i + 1][0]
            off += n
        kmax_scr[...] = jnp.zeros(kmax_scr.shape, F32)

        def knorm(k, vt):
            kf = k.astype(F32)
            n2 = jnp.max(jnp.sum(kf * kf, axis=1, keepdims=True), axis=0, keepdims=True)
            kmax_scr[...] = jnp.maximum(kmax_scr[...], n2)
        rolled(knorm)

    q4 = jnp.concatenate([q_ref[0, :, g * 128:(g + 1) * 128] for g in range(Q_PER_KV)], axis=0)
    qf = q4.astype(F32)
    qn2 = _dot_nt(jnp.ones((SUBLANES, HEAD_DIM), BF16), (qf * qf).astype(BF16))[0:1, :]
    shift = jnp.sqrt(qn2 * kmax_scr[0:1, 0:1])

    conv_fill, conv_pieces = _dwconv(y_ref, yp_ref, yn_ref, wdw_ref, conv_ref, ybuf, ysh,
                                     conv_blk == 0, conv_blk == n_conv_blk - 1)
    conv_fill()
    per_chunk = -(-len(conv_pieces) // max(n_chunk - 2, 1))

    acc = jnp.zeros((V_ROWS, M), F32)
    piece_zeros = {}
    for c in range(n_chunk):
        shift_c = shift
        for zero in piece_zeros.pop(c, []):
            shift_c = jnp.concatenate([shift_c[:, :128] + zero[0:1, :], shift_c[:, 128:]], axis=1)
        rows = slice(c * tk, (c + 1) * tk)
        st = _dot_nt(k_scr[rows, :], q4)
        acc = acc + _dot(vt_scr[:, rows], jnp.exp2(st - shift_c).astype(BF16))
        for piece in conv_pieces[c * per_chunk:(c + 1) * per_chunk]:
            zero = piece(_zero_after(acc[0:1, 0:128]))
            if c + CONV_LAG < n_chunk:
                piece_zeros.setdefault(c + CONV_LAG, []).append(zero)
    acc_scr[...] = acc

    def write_out():
        a = acc_scr[...]
        ot = a[:HEAD_DIM, :] / a[HEAD_DIM:HEAD_DIM + 1, :]
        for g in range(Q_PER_KV):
            o_ref[0, :, g * 128:(g + 1) * 128] = jnp.transpose(ot[:, g * tq:(g + 1) * tq]).astype(BF16)
    write_out()

    l_min = jnp.min(acc_scr[HEAD_DIM:HEAD_DIM + 1, :])

    @pl.when(jnp.logical_not(l_min >= L_MIN))
    def _():
        m_scr[...] = jnp.full(m_scr.shape, -jnp.inf, F32)
        acc_scr[...] = jnp.zeros(acc_scr.shape, F32)

        def online(k, vt):
            st = _dot_nt(k, q4)
            m_prev = m_scr[0:1, :]
            m_new = jnp.maximum(m_prev, jnp.max(st, axis=0, keepdims=True))
            alpha = jnp.exp2(m_prev - m_new)
            acc_scr[...] = alpha * acc_scr[...] + _dot(vt, jnp.exp2(st - m_new).astype(BF16))
            m_scr[...] = jnp.broadcast_to(m_new, m_scr.shape)
        rolled(online)
        write_out()


def _attention(q, srcs, y, w_dw, l, *, tq, tk):
    B, Nq, D = y.shape
    src_len = tuple(k.shape[1] for k, _ in srcs)
    lk = sum(src_len)
    assert lk % tk == 0
    M = Q_PER_KV * tq
    nq = Nq // tq
    cr = tq // N_KV_HEADS
    hb = cr // HALO
    n_halo = Nq // HALO
    conv_blk = lambda h, i: h * nq + i
    in_specs = [pl.BlockSpec((1, tq, Q_PER_KV * HEAD_DIM), lambda b, h, i: (b, i, h))]
    args = [q]
    for (k, vt), n in zip(srcs, src_len):
        in_specs += [pl.BlockSpec((1, n, HEAD_DIM), lambda b, h, i: (b, 0, h)),
                     pl.BlockSpec((1, V_ROWS, n), lambda b, h, i: (b, h, 0))]
        args += [k, vt]
    in_specs += [pl.BlockSpec((1, cr, D), lambda b, h, i: (b, conv_blk(h, i), 0)),
                 pl.BlockSpec((1, HALO, D), lambda b, h, i: (b, jnp.maximum(conv_blk(h, i) * hb - 1, 0), 0)),
                 pl.BlockSpec((1, HALO, D),
                              lambda b, h, i: (b, jnp.minimum((conv_blk(h, i) + 1) * hb, n_halo - 1), 0)),
                 _layer_resident((CONV_ROWS, D), l)]
    args += [y, y, y, w_dw]
    return pl.pallas_call(
        functools.partial(_attn_kernel, src_len=src_len, tq=tq, tk=tk),
        out_shape=[jax.ShapeDtypeStruct((B, Nq, ATT_Q), BF16), jax.ShapeDtypeStruct((B, Nq, D), F32)],
        grid=(B, N_KV_HEADS, nq),
        in_specs=in_specs,
        out_specs=[pl.BlockSpec((1, tq, Q_PER_KV * HEAD_DIM), lambda b, h, i: (b, i, h)),
                   pl.BlockSpec((1, cr, D), lambda b, h, i: (b, conv_blk(h, i), 0))],
        scratch_shapes=[pltpu.VMEM((lk, HEAD_DIM), BF16), pltpu.VMEM((V_ROWS, lk), BF16),
                        pltpu.VMEM((SUBLANES, 128), F32), pltpu.VMEM((V_ROWS, M), F32),
                        pltpu.VMEM((SUBLANES, M), F32),
                        pltpu.VMEM((cr + 2 * HALO, D), F32),
                        pltpu.VMEM((SUBLANES - 1, cr + 2 * HALO - SUBLANES, D), F32)],
        compiler_params=_cparams(("arbitrary", "arbitrary", "arbitrary")),
        name="attn%d" % len(srcs),
    )(*args)


def _gla_kernel(qf_ref, kf_ref, vf_ref, bf_ref, qb_ref, kb_ref, vb_ref, bb_ref, tri_ref, s0_ref,
                of_ref, ob_ref, send_ref, s_scr, *, n_chunk):
    j = pl.program_id(1)

    @pl.when(j == 0)
    def _():
        s_scr[...] = s0_ref[:, 0]

    dirs = ((qf_ref, kf_ref, vf_ref, bf_ref, of_ref), (qb_ref, kb_ref, vb_ref, bb_ref, ob_ref))

    def intra_exact(q, k, b):
        row_id = lax.broadcasted_iota(jnp.int32, (GLA_C, GLA_DK), 0)
        col_id = lax.broadcasted_iota(jnp.int32, (GLA_C, GLA_C), 1)

        def one(jrow, a):
            pick = row_id == jrow
            k_j = jnp.sum(jnp.where(pick, k, 0.0), axis=0, keepdims=True)
            b_j = jnp.sum(jnp.where(pick, b, 0.0), axis=0, keepdims=True)
            col = jnp.sum(q * k_j * jnp.exp(jnp.minimum(b - b_j, 0.0)), axis=1, keepdims=True)
            return jnp.where(col_id == jrow, col, a)
        return lax.fori_loop(0, GLA_C, one, jnp.zeros((GLA_C, GLA_C), F32))

    def make_body(exact):
        def body(i, carry):
            for d, (q_ref, k_ref, v_ref, b_ref, o_ref) in enumerate(dirs):
                c = i if d == 0 else n_chunk - 1 - i
                rows = pl.ds(pl.multiple_of(c * GLA_C, GLA_C), GLA_C)
                tri = tri_ref[d]
                for h in range(GLA_HEADS):
                    kc = slice(h * GLA_DK, (h + 1) * GLA_DK)
                    vc = slice(h * GLA_DV, (h + 1) * GLA_DV)
                    b = b_ref[0, 0, rows, kc]
                    q = q_ref[0, rows, kc].astype(F32)
                    k = k_ref[0, rows, kc].astype(F32)
                    v = v_ref[0, rows, vc]
                    b_end = jnp.min(b, axis=0, keepdims=True)
                    qe = q * jnp.exp(b)
                    kh = k * jnp.exp(b_end - b)
                    if exact:
                        a = intra_exact(q, k, b)
                    else:
                        mid = jnp.exp(-0.5 * b_end)
                        a = _dot_nt((qe * mid).astype(BF16), (kh * mid).astype(BF16))
                    qe = qe.astype(BF16)
                    a = a * tri
                    av = _dot(jnp.concatenate([a.astype(BF16), jnp.transpose(kh).astype(BF16)], axis=0), v)
                    s = s_scr[d, h]
                    o_ref[0, rows, vc] = (_dot(qe, s.astype(BF16)) + av[:GLA_C]).astype(BF16)
                    e_col = jnp.transpose(jnp.broadcast_to(jnp.exp(b_end), (GLA_DK, GLA_DK)))
                    s_scr[d, h] = jnp.concatenate([e_col, e_col], axis=1) * s + av[GLA_C:]
            return carry
        return body

    safe = jnp.minimum(jnp.min(bf_ref[...]), jnp.min(bb_ref[...])) >= -DECAY_SAFE

    @pl.when(safe)
    def _():
        lax.fori_loop(0, n_chunk, make_body(False), 0, unroll=2)

    @pl.when(jnp.logical_not(safe))
    def _():
        lax.fori_loop(0, n_chunk, make_body(True), 0)

    @pl.when(j == pl.num_programs(1) - 1)
    def _():
        send_ref[:, 0] = s_scr[...]


def _gla(gq, gk, gv, bc, s0, tri2, *, tb):
    B, N, _ = gq.shape
    nb = N // tb
    fwd = lambda w: pl.BlockSpec((1, tb, w), lambda b, j: (b, j, 0))
    bwd = lambda w: pl.BlockSpec((1, tb, w), lambda b, j: (b, nb - 1 - j, 0))
    state = pl.BlockSpec((2, 1, GLA_HEADS, GLA_DK, GLA_DV), lambda b, j: (0, b, 0, 0, 0))
    return pl.pallas_call(
        functools.partial(_gla_kernel, n_chunk=tb // GLA_C),
        out_shape=[jax.ShapeDtypeStruct((B, N, GLA_V), BF16), jax.ShapeDtypeStruct((B, N, GLA_V), BF16),
                   jax.ShapeDtypeStruct((2, B, GLA_HEADS, GLA_DK, GLA_DV), F32)],
        grid=(B, nb),
        in_specs=[fwd(GLA_K), fwd(GLA_K), fwd(GLA_V),
                  pl.BlockSpec((1, 1, tb, GLA_K), lambda b, j: (0, b, j, 0)),
                  bwd(GLA_K), bwd(GLA_K), bwd(GLA_V),
                  pl.BlockSpec((1, 1, tb, GLA_K), lambda b, j: (1, b, nb - 1 - j, 0)),
                  _resident((2, GLA_C, GLA_C)), state],
        out_specs=[fwd(GLA_V), bwd(GLA_V), state],
        scratch_shapes=[pltpu.VMEM((2, GLA_HEADS, GLA_DK, GLA_DV), F32)],
        compiler_params=_cparams(("arbitrary", "arbitrary")),
        name="gla",
    )(gq, gk, gv, bc, gq, gk, gv, bc, tri2, s0)


def _merge_kernel(x_ref, mod_ref, oatt_ref, ogf_ref, ogb_ref, r_ref, conv_ref, gates_ref,
                  gn_ref, bdw_ref, clg_ref, clb_ref, watt_ref, wgla_ref, wconv_ref, wout_ref,
                  lng_ref, lnb_ref, xo_ref, *, sub):
    for r0 in range(0, x_ref.shape[1], sub):
        rows = slice(r0, r0 + sub)
        yc = _ln(conv_ref[0, rows, :] + bdw_ref[...]) * clg_ref[...] + clb_ref[...]
        y_conv = _dot(_silu(yc).astype(BF16), wconv_ref[...])

        y_att = _dot(oatt_ref[0, rows, :], watt_ref[...])

        parts = []
        for h in range(GLA_HEADS):
            vc = slice(h * GLA_DV, (h + 1) * GLA_DV)
            o = ogf_ref[0, rows, vc].astype(F32) + ogb_ref[0, rows, vc].astype(F32)
            o = o * lax.rsqrt(jnp.mean(o * o, axis=-1, keepdims=True) + NORM_EPS) * gn_ref[...]
            parts.append((o * r_ref[0, rows, vc].astype(F32)).astype(BF16))
        y_gla = _dot(jnp.concatenate(parts, axis=1), wgla_ref[...])

        g_att = gates_ref[0, rows, 0:D_MODEL].astype(F32)
        g_gla = gates_ref[0, rows, D_MODEL:2 * D_MODEL].astype(F32)
        g_conv = gates_ref[0, rows, 2 * D_MODEL:].astype(F32)
        mix = g_att * y_att + g_gla * y_gla + g_conv * y_conv
        f = _dot(mix.astype(BF16), wout_ref[...])
        gate = mod_ref[0, 2:3, :]
        xo_ref[0, rows, :] = _ln(ALPHA * x_ref[0, rows, :] + gate * f) * lng_ref[...] + lnb_ref[...]


def _merge(x, mod, o_att, o_gla_f, o_gla_b, r, conv, gates, p, l, *, tm):
    B, N, D = x.shape
    row = lambda w: pl.BlockSpec((1, tm, w), lambda b, i: (b, i, 0))
    res = lambda shape: _layer_resident(shape, l)
    vec = res((1, D))
    return pl.pallas_call(
        functools.partial(_merge_kernel, sub=min(SUB_TILE, tm)),
        out_shape=jax.ShapeDtypeStruct((B, N, D), F32),
        grid=(B, N // tm),
        in_specs=[row(D),
                  pl.BlockSpec((1, 6, D), lambda b, i: (b, 0, 0)),
                  row(ATT_Q), row(GLA_V), row(GLA_V), row(GLA_V), row(D), row(3 * D),
                  res((1, GLA_DV)), vec, vec, vec,
                  res((ATT_Q, D)), res((GLA_V, D)), res((D, D)), res((D, D)),
                  vec, vec],
        out_specs=row(D),
        compiler_params=_cparams(("arbitrary", "arbitrary")),
        name="merge",
    )(x, mod, o_att, o_gla_f, o_gla_b, r, conv, gates,
      p['gla_norm'], p['conv_b_dw'], p['conv_ln_g'], p['conv_ln_b'],
      p['w_att_o'], p['w_gla_o'], p['w_conv_o'], p['w_out'], p['ln1_g'], p['ln1_b'])


def _ffn_kernel(x_ref, mod_ref, wg_ref, wu_ref, wd_ref, lng_ref, lnb_ref, xo_ref, h_scr, a_scr, *, tf, sub):
    for r0 in range(0, x_ref.shape[1], sub):
        rows = slice(r0, r0 + sub)
        x = x_ref[0, rows, :]
        h_scr[rows, :] = (_ln(x) * (1.0 + mod_ref[0, 4:5, :]) + mod_ref[0, 3:4, :]).astype(BF16)
        for c0 in range(0, D_FF, tf):
            cols = slice(c0, min(c0 + tf, D_FF))
            g = _dot(h_scr[rows, :], wg_ref[:, cols])
            u = _dot(h_scr[rows, :], wu_ref[:, cols])
            a_scr[rows, cols] = (_silu(g) * u).astype(BF16)
        f = _dot(a_scr[rows, :], wd_ref[...])
        xo_ref[0, rows, :] = _ln(ALPHA * x + mod_ref[0, 5:6, :] * f) * lng_ref[...] + lnb_ref[...]


def _ffn(x, mod, p, l, *, tm):
    B, N, D = x.shape
    row = pl.BlockSpec((1, tm, D), lambda b, i: (b, i, 0))
    res = lambda shape: _layer_resident(shape, l)
    return pl.pallas_call(
        functools.partial(_ffn_kernel, tf=FFN_COLS, sub=min(SUB_TILE, tm)),
        out_shape=jax.ShapeDtypeStruct((B, N, D), F32),
        grid=(B, N // tm),
        in_specs=[row, pl.BlockSpec((1, 6, D), lambda b, i: (b, 0, 0)),
                  res((D, D_FF)), res((D, D_FF)), res((D_FF, D)), res((1, D)), res((1, D))],
        out_specs=row,
        scratch_shapes=[pltpu.VMEM((tm, D), BF16), pltpu.VMEM((tm, D_FF), BF16)],
        compiler_params=_cparams(("arbitrary", "arbitrary")),
        name="ffn",
    )(x, mod, p['w_ff_gate'], p['w_ff_up'], p['w_ff_down'], p['ln2_g'], p['ln2_b'])


def _tri_blocks(tm):
    idx = np.arange(tm)
    same = (idx[:, None] // GLA_C) == (idx[None, :] // GLA_C)
    tril = same & (idx[None, :] <= idx[:, None])
    triu = same & (idx[None, :] >= idx[:, None])
    return jnp.asarray(tril, BF16), jnp.asarray(triu, BF16)


def _rope_tables(S):
    t = np.arange(S)
    row = (t // GRID_W).astype(np.float32)
    col = (t % GRID_W).astype(np.float32)
    half = HEAD_DIM // 2
    inv = (ROPE_THETA ** (-np.arange(0, half, 2, dtype=np.float32) / half)).astype(np.float32)
    ang = np.concatenate([row[:, None] * inv, col[:, None] * inv], axis=-1).astype(np.float32)
    cos, sin = np.cos(ang), np.sin(ang)
    return (jnp.asarray(np.repeat(cos, 2, axis=-1), F32),
            jnp.asarray(np.stack([-sin, sin], axis=-1).reshape(S, HEAD_DIM), F32))


def _stacked_params(w_in, q_norm, k_norm, w_att_o, gla_w_a2, gla_b_a, gla_norm, w_gla_o, conv_w_dw,
                    conv_b_dw, conv_ln_g, conv_ln_b, w_conv_o, w_out, ln1_g, ln1_b, w_ff_gate, w_ff_up,
                    w_ff_down, ln2_g, ln2_b):
    L = w_in.shape[0]
    glr0 = C_MEM_END
    glr1 = glr0 + 2 * GLA_RANK
    v_w = w_in[:, :, ATT_KV:2 * ATT_KV]
    zero = jnp.zeros((L, GLA_RANK, GLA_K), F32)
    w_a2 = jnp.concatenate([jnp.concatenate([gla_w_a2[:, 0], zero], axis=2),
                            jnp.concatenate([zero, gla_w_a2[:, 1]], axis=2)], axis=1).astype(BF16)
    vec = lambda a: a.reshape(L, 1, -1)
    return {
        'w_mem': w_in[:, :, :glr0].astype(BF16), 'w_rest': w_in[:, :, glr1:].astype(BF16),
        'w_vt': jnp.swapaxes(v_w, 1, 2).astype(BF16), 'w_glr': w_in[:, :, glr0:glr1].astype(BF16),
        'w_a2': w_a2, 'b_a': vec(gla_b_a), 'q_norm': vec(q_norm), 'k_norm': vec(k_norm),
        'w_att_o': w_att_o.astype(BF16), 'gla_norm': vec(gla_norm), 'w_gla_o': w_gla_o.astype(BF16),
        'conv_w_dw': jnp.pad(conv_w_dw[:, :, 0, :], ((0, 0), (0, CONV_ROWS - CONV_W), (0, 0))),
        'conv_b_dw': vec(conv_b_dw), 'conv_ln_g': vec(conv_ln_g), 'conv_ln_b': vec(conv_ln_b),
        'w_conv_o': w_conv_o.astype(BF16), 'w_out': w_out.astype(BF16),
        'ln1_g': vec(ln1_g), 'ln1_b': vec(ln1_b),
        'w_ff_gate': w_ff_gate.astype(BF16), 'w_ff_up': w_ff_up.astype(BF16),
        'w_ff_down': w_ff_down.astype(BF16), 'ln2_g': vec(ln2_g), 'ln2_b': vec(ln2_b),
    }


def _key_chunk(n_keys):
    return max(t for t in range(MXU_TILE, 5 * MXU_TILE, MXU_TILE) if n_keys % t == 0)


def kernel(x, c, ctx, c_ctx, w_ada, b_ada, w_in, q_norm, k_norm, w_att_o, gla_w_a2, gla_b_a, gla_norm,
           w_gla_o, conv_w_dw, conv_b_dw, conv_ln_g, conv_ln_b, w_conv_o, w_out, ln1_g, ln1_b,
           w_ff_gate, w_ff_up, w_ff_down, ln2_g, ln2_b):
    B, S, D = x.shape
    LC = ctx.shape[1]
    tm_lat = min(ROW_TILE, S)
    tm_ctx = min(SUB_TILE, LC)
    tq_ctx = min(ATT_TQ // 2, LC)
    cos, sin = _rope_tables(S)
    tri_sizes = sorted({min(SUB_TILE, tm_lat), min(SUB_TILE, tm_ctx)})
    tabs_lat = {'cos': cos, 'sin': sin, 'tri': {t: _tri_blocks(t) for t in tri_sizes}}
    tabs_ctx = {'cos': cos[:LC], 'sin': sin[:LC], 'tri': tabs_lat['tri']}
    idx = np.arange(GLA_C)
    tri2 = jnp.asarray(np.stack([idx[None, :] <= idx[:, None], idx[None, :] >= idx[:, None]]), F32)

    assert B + 1 <= ADALN_ROWS
    cvec = jnp.concatenate([c, c_ctx[None, :], jnp.zeros((ADALN_ROWS - B - 1, D), F32)], axis=0)
    ada = _adaln(cvec, w_ada, b_ada)
    zero_state = jnp.zeros((2, B, GLA_HEADS, GLA_DK, GLA_DV), F32)

    p = _stacked_params(w_in, q_norm, k_norm, w_att_o, gla_w_a2, gla_b_a, gla_norm, w_gla_o, conv_w_dw,
                        conv_b_dw, conv_ln_g, conv_ln_b, w_conv_o, w_out, ln1_g, ln1_b, w_ff_gate,
                        w_ff_up, w_ff_down, ln2_g, ln2_b)
    xc = ctx
    for l in range(DEPTH):
        mod = ada[l, :B].reshape(B, 6, D)
        mod_c = jnp.broadcast_to(ada[l, B].reshape(1, 6, D), (B, 6, D))
        last = l == DEPTH - 1

        qc, kc, vc, gkc, gvc, gqc, rc, yc, gatesc, bcc = _inproj(xc, mod_c, tabs_ctx, p, l, rope=False,
                                                                 tm=tm_ctx)
        ogf_c, ogb_c, s_ctx = _gla(gqc, gkc, gvc, bcc, zero_state, tri2, tb=tm_ctx)
        if not last:
            o_att_c, conv_c = _attention(qc, [(kc, vc)], yc, p['conv_w_dw'], l, tq=tq_ctx,
                                         tk=_key_chunk(LC))
            xc = _merge(xc, mod_c, o_att_c, ogf_c, ogb_c, rc, conv_c, gatesc, p, l, tm=tm_ctx)
            xc = _ffn(xc, mod_c, p, l, tm=tm_ctx)

        q, k, v, gk, gv, gq, r, y, gates, bc = _inproj(x, mod, tabs_lat, p, l, rope=True, tm=tm_lat)
        o_att, conv = _attention(q, [(k, v), (kc, vc)], y, p['conv_w_dw'], l, tq=min(ATT_TQ, S),
                                 tk=_key_chunk(S + LC))
        ogf, ogb, _ = _gla(gq, gk, gv, bc, s_ctx, tri2, tb=min(GLA_TB, S))
        x = _merge(x, mod, o_att, ogf, ogb, r, conv, gates, p, l, tm=tm_lat)
        x = _ffn(x, mod, p, l, tm=tm_lat)
    return x
```
